```python
import jax, jax.numpy as jnp
from jax import lax
import numpy as np

D_MODEL = 1024
BATCH = 2
SEQ = 16384
DEPTH = 4

N_MIXERS = 2
ATTN_HEAD_DIM = 64
ATTN_HEADS = D_MODEL // ATTN_HEAD_DIM
DILATION_PAIRS = ((128, 1), (512, 4), (2048, 16))
ATTN_BLOCK = 128
MLSTM_HEADS = 4
MLSTM_V_DIM = D_MODEL // MLSTM_HEADS
MLSTM_QK_DIM = MLSTM_V_DIM // 2
MLSTM_CHUNK = 64
CONV_WIDTH = 4
N_GROUPS = 8
EXPERTS_PER_GROUP = 8
N_EXPERTS = N_GROUPS * EXPERTS_PER_GROUP
TOP_K_IN_GROUP = 2
EXPERT_FF = D_MODEL // 2
MOE_BLOCK = 256
NORM_EPS = 1e-6

kernel_name = "hybrid_dilated_attn_mlstm_hmoe"

F32 = jnp.float32


def rms_norm(x, gain):
    xf = x.astype(F32)
    y = xf * lax.rsqrt(jnp.mean(xf * xf, axis=-1, keepdims=True) + NORM_EPS)
    return (y * gain.astype(F32)).astype(x.dtype)


def rms_norm_f32(x, gain):
    xf = x.astype(F32)
    return xf * lax.rsqrt(jnp.mean(xf * xf, axis=-1, keepdims=True) + NORM_EPS) * gain.astype(F32)


def dilated_branch(q, k, v, window, dilation):
    B_, S_, H_, Dh = q.shape
    n_back = window // dilation
    blk = ATTN_BLOCK
    span = dilation * blk
    S_pad = -(-S_ // span) * span
    nb = S_pad // span
    pad = ((0, 0), (0, S_pad - S_), (0, 0), (0, 0))

    def split(t):
        return jnp.pad(t, pad).reshape(B_, nb, blk, dilation, H_, Dh)

    def band(t):
        prev = jnp.concatenate([jnp.zeros_like(t[:, :1]), t[:, :-1]], axis=1)
        return jnp.concatenate([prev, t], axis=2)

    qb = split(q)
    kband = band(split(k))
    vband = band(split(v))
    s = jnp.einsum('bnqrhd,bnkrhd->bnrhqk', qb, kband) * (Dh ** -0.5)
    qi = jnp.arange(blk)[:, None]
    ki = jnp.arange(2 * blk)[None, :]
    dist = blk + qi - ki
    valid = (dist >= 0) & (dist <= n_back)
    valid = valid[None] & ((jnp.arange(nb)[:, None, None] > 0) | (ki >= blk)[None])
    s = jnp.where(valid[None, :, None, None], s, -jnp.inf)
    mx = jnp.max(s, axis=-1, keepdims=True)
    p = jnp.exp(s - mx)
    l = jnp.sum(p, axis=-1, keepdims=True)
    o = jnp.einsum('bnrhqk,bnkrhd->bnrhqd', p, vband) / l
    o = o.transpose(0, 1, 4, 2, 3, 5).reshape(B_, S_pad, H_, Dh)[:, :S_]
    lse = (mx[..., 0] + jnp.log(l[..., 0])).transpose(0, 1, 4, 2, 3).reshape(B_, S_pad, H_)[:, :S_]
    return o, lse


def dilated_attention_mixer(h, w_in, q_gain, k_gain, w_out):
    B_, S_, _ = h.shape
    qkv = (h @ w_in).reshape(B_, S_, 3, ATTN_HEADS, ATTN_HEAD_DIM).astype(F32)
    q = rms_norm_f32(qkv[:, :, 0], q_gain)
    k = rms_norm_f32(qkv[:, :, 1], k_gain)
    v = qkv[:, :, 2]
    outs, lses = [], []
    for window, dilation in DILATION_PAIRS:
        o, l = dilated_branch(q, k, v, window, dilation)
        outs.append(o)
        lses.append(l)
    wts = jax.nn.softmax(jnp.stack(lses), axis=0)
    o = jnp.einsum('gbsh,gbshd->bshd', wts, jnp.stack(outs))
    return o.reshape(B_, S_, ATTN_HEADS * ATTN_HEAD_DIM).astype(h.dtype) @ w_out


def causal_depthwise_conv(x, w):
    c = x.shape[-1]
    return lax.conv_general_dilated(
        x, w[:, None, :].astype(x.dtype), window_strides=(1,), padding=((CONV_WIDTH - 1, 0),),
        dimension_numbers=('NWC', 'WIO', 'NWC'), feature_group_count=c)


def mlstm_chunkwise(q, k, v, logi, logf):
    B_, S_, NH, DK = q.shape
    DV = v.shape[-1]
    L = MLSTM_CHUNK
    nc = S_ // L

    def chunks4(t):
        return t.reshape(B_, nc, L, NH, t.shape[-1]).transpose(1, 0, 3, 2, 4)

    def chunks3(t):
        return t.reshape(B_, nc, L, NH).transpose(1, 0, 3, 2)

    causal = jnp.tril(jnp.ones((L, L), dtype=bool))

    def step(carry, inp):
        C, n, m = carry
        qc, kc, vc, ic, fc = inp
        b = jnp.cumsum(fc, axis=-1)
        D = jnp.where(causal, b[..., :, None] - b[..., None, :] + ic[..., None, :], -jnp.inf)
        m_inter = b + m[..., None]
        m_t = jnp.maximum(m_inter, jnp.max(D, axis=-1))
        s = jnp.einsum('bhtd,bhsd->bhts', qc, kc) * jnp.exp(D - m_t[..., None])
        w_inter = jnp.exp(m_inter - m_t)
        num = jnp.einsum('bhts,bhsv->bhtv', s, vc) + w_inter[..., None] * jnp.einsum('bhtd,bhdv->bhtv', qc, C)
        den = jnp.sum(s, axis=-1) + w_inter * jnp.einsum('bhtd,bhd->bht', qc, n)
        h = num / jnp.maximum(jnp.abs(den), jnp.exp(-m_t))[..., None]
        bL = b[..., -1]
        g = bL[..., None] - b + ic
        m_new = jnp.maximum(bL + m, jnp.max(g, axis=-1))
        decay = jnp.exp(bL + m - m_new)
        wk = jnp.exp(g - m_new[..., None])[..., None] * kc
        C_new = decay[..., None, None] * C + jnp.einsum('bhsd,bhsv->bhdv', wk, vc)
        n_new = decay[..., None] * n + jnp.sum(wk, axis=-2)
        return (C_new, n_new, m_new), h

    init = (jnp.zeros((B_, NH, DK, DV), F32), jnp.zeros((B_, NH, DK), F32), jnp.zeros((B_, NH), F32))
    _, hs = lax.scan(step, init, (chunks4(q), chunks4(k), chunks4(v), chunks3(logi), chunks3(logf)))
    return hs.transpose(1, 0, 3, 2, 4).reshape(B_, S_, NH, DV)


def mlstm_mixer(h, w_in, conv_w, conv_b, gate_b, h_gain, w_out):
    B_, S_, _ = h.shape
    qk_w = MLSTM_HEADS * MLSTM_QK_DIM
    v_w = MLSTM_HEADS * MLSTM_V_DIM
    proj = h @ w_in
    qk = proj[..., :2 * qk_w]
    v = proj[..., 2 * qk_w:2 * qk_w + v_w]
    og = proj[..., 2 * qk_w + v_w:2 * qk_w + 2 * v_w]
    gates = proj[..., 2 * qk_w + 2 * v_w:].astype(F32) + gate_b.astype(F32)
    qk = jax.nn.silu(causal_depthwise_conv(qk, conv_w) + conv_b.astype(qk.dtype)).astype(F32)
    q = qk[..., :qk_w].reshape(B_, S_, MLSTM_HEADS, MLSTM_QK_DIM)
    k = qk[..., qk_w:].reshape(B_, S_, MLSTM_HEADS, MLSTM_QK_DIM) * (MLSTM_QK_DIM ** -0.5)
    v = v.astype(F32).reshape(B_, S_, MLSTM_HEADS, MLSTM_V_DIM)
    logi = gates[..., :MLSTM_HEADS]
    logf = jax.nn.log_sigmoid(gates[..., MLSTM_HEADS:])
    hs = mlstm_chunkwise(q, k, v, logi, logf)
    hs = rms_norm_f32(hs, h_gain.reshape(MLSTM_HEADS, MLSTM_V_DIM))
    y = hs.reshape(B_, S_, v_w) * jax.nn.sigmoid(og.astype(F32))
    return y.astype(h.dtype) @ w_out


def hierarchical_moe(h, w_group, b_group, w_expert, b_expert, w_gate, w_up, w_down):
    B_, S_, D_ = h.shape
    N = B_ * S_
    xf = h.reshape(N, D_)
    g_logits = (xf @ w_group).astype(F32) + b_group.astype(F32)
    g_prob = jax.nn.softmax(g_logits, axis=-1)
    _, grp = lax.top_k(g_logits, 1)
    grp = grp[:, 0]
    p_grp = jnp.take_along_axis(g_prob, grp[:, None], axis=-1)[:, 0]
    e_logits = ((xf @ w_expert).astype(F32) + b_expert.astype(F32)).reshape(N, N_GROUPS, EXPERTS_PER_GROUP)
    e_in = jnp.take_along_axis(e_logits, grp[:, None, None], axis=1)[:, 0]
    top_v, top_i = lax.top_k(e_in, TOP_K_IN_GROUP)
    gates = p_grp[:, None] * jax.nn.softmax(top_v, axis=-1)
    expert_id = grp[:, None] * EXPERTS_PER_GROUP + top_i

    A = N * TOP_K_IN_GROUP
    e_flat = expert_id.reshape(A).astype(jnp.int32)
    tok_flat = jnp.repeat(jnp.arange(N, dtype=jnp.int32), TOP_K_IN_GROUP)
    g_flat = gates.reshape(A)
    order = jnp.argsort(e_flat)
    e_s, tok_s, g_s = e_flat[order], tok_flat[order], g_flat[order]
    counts = jnp.bincount(e_flat, length=N_EXPERTS)
    padded = ((counts + MOE_BLOCK - 1) // MOE_BLOCK) * MOE_BLOCK
    pad_end = jnp.cumsum(padded)
    pad_start = pad_end - padded
    start = jnp.cumsum(counts) - counts
    dest = pad_start[e_s] + jnp.arange(A, dtype=jnp.int32) - start[e_s]
    P = (A // MOE_BLOCK + N_EXPERTS) * MOE_BLOCK
    nb = P // MOE_BLOCK
    row_tok = jnp.full((P,), N, dtype=jnp.int32).at[dest].set(tok_s)
    row_gate = jnp.zeros((P,), F32).at[dest].set(g_s)
    block_exp = jnp.clip(jnp.searchsorted(pad_end, jnp.arange(nb) * MOE_BLOCK, side='right'), 0, N_EXPERTS - 1)
    x_pad = jnp.concatenate([xf, jnp.zeros((1, D_), xf.dtype)], axis=0)
    xs = x_pad[row_tok].reshape(nb, MOE_BLOCK, D_)

    def expert_block(args):
        xb, e = args
        return (jax.nn.silu(xb @ w_gate[e]) * (xb @ w_up[e])) @ w_down[e]

    ys = lax.map(expert_block, (xs, block_exp)).reshape(P, D_)
    out = jnp.zeros((N + 1, D_), F32).at[row_tok].add(ys.astype(F32) * row_gate[:, None])[:N]
    return out.reshape(B_, S_, D_).astype(h.dtype)


def setup_inputs(seed: int = 0) -> dict:
    key = jax.random.key(seed)
    ks = jax.random.split(key, 24)
    n_attn = (DEPTH + N_MIXERS - 1) // N_MIXERS
    n_mlstm = DEPTH // N_MIXERS
    d = D_MODEL
    attn_w = ATTN_HEADS * ATTN_HEAD_DIM
    qk_w = MLSTM_HEADS * MLSTM_QK_DIM
    v_w = MLSTM_HEADS * MLSTM_V_DIM
    nrm = jax.random.normal
    return {
        "x": nrm(ks[0], (BATCH, SEQ, d), F32),
        "norm_mix": 1.0 + 0.02 * nrm(ks[1], (DEPTH, d), F32),
        "norm_ffn": 1.0 + 0.02 * nrm(ks[2], (DEPTH, d), F32),
        "attn_w_in": nrm(ks[3], (n_attn, d, 3 * attn_w), F32) * d ** -0.5,
        "attn_q_norm": 1.0 + 0.02 * nrm(ks[4], (n_attn, ATTN_HEAD_DIM), F32),
        "attn_k_norm": 1.0 + 0.02 * nrm(ks[5], (n_attn, ATTN_HEAD_DIM), F32),
        "attn_w_out": nrm(ks[6], (n_attn, attn_w, d), F32) * attn_w ** -0.5,
        "mlstm_w_in": nrm(ks[7], (n_mlstm, d, 2 * qk_w + 2 * v_w + 2 * MLSTM_HEADS), F32) * d ** -0.5,
        "mlstm_conv_w": nrm(ks[8], (n_mlstm, CONV_WIDTH, 2 * qk_w), F32) * CONV_WIDTH ** -0.5,
        "mlstm_conv_b": 0.02 * nrm(ks[9], (n_mlstm, 2 * qk_w), F32),
        "mlstm_gate_b": jnp.concatenate([
            0.1 * nrm(ks[10], (n_mlstm, MLSTM_HEADS), F32),
            jax.random.uniform(ks[11], (n_mlstm, MLSTM_HEADS), F32, 3.0, 6.0)], axis=-1),
        "mlstm_h_norm": 1.0 + 0.02 * nrm(ks[12], (n_mlstm, v_w), F32),
        "mlstm_w_out": nrm(ks[13], (n_mlstm, v_w, d), F32) * v_w ** -0.5,
        "moe_w_group": nrm(ks[14], (DEPTH, d, N_GROUPS), F32) * d ** -0.5,
        "moe_b_group": 0.01 * nrm(ks[15], (DEPTH, N_GROUPS), F32),
        "moe_w_expert": nrm(ks[16], (DEPTH, d, N_EXPERTS), F32) * d ** -0.5,
        "moe_b_expert": 0.01 * nrm(ks[17], (DEPTH, N_EXPERTS), F32),
        "moe_w_gate": nrm(ks[18], (DEPTH, N_EXPERTS, d, EXPERT_FF), F32) * d ** -0.5,
        "moe_w_up": nrm(ks[19], (DEPTH, N_EXPERTS, d, EXPERT_FF), F32) * d ** -0.5,
        "moe_w_down": nrm(ks[20], (DEPTH, N_EXPERTS, EXPERT_FF, d), F32) * EXPERT_FF ** -0.5,
    }


def reference(x, norm_mix, norm_ffn, attn_w_in, attn_q_norm, attn_k_norm, attn_w_out,
              mlstm_w_in, mlstm_conv_w, mlstm_conv_b, mlstm_gate_b, mlstm_h_norm, mlstm_w_out,
              moe_w_group, moe_b_group, moe_w_expert, moe_b_expert, moe_w_gate, moe_w_up, moe_w_down):
    for layer in range(DEPTH):
        j = layer // N_MIXERS
        h = rms_norm(x, norm_mix[layer])
        if layer % N_MIXERS == 0:
            x = x + dilated_attention_mixer(h, attn_w_in[j], attn_q_norm[j], attn_k_norm[j], attn_w_out[j])
        else:
            x = x + mlstm_mixer(h, mlstm_w_in[j], mlstm_conv_w[j], mlstm_conv_b[j], mlstm_gate_b[j],
                                mlstm_h_norm[j], mlstm_w_out[j])
        h = rms_norm(x, norm_ffn[layer])
        x = x + hierarchical_moe(h, moe_w_group[layer], moe_b_group[layer], moe_w_expert[layer],
                                 moe_b_expert[layer], moe_w_gate[layer], moe_w_up[layer], moe_w_down[layer])
    return x
```

```python
import functools

import jax
import jax.numpy as jnp
from jax import lax
from jax.experimental import pallas as pl
from jax.experimental.pallas import tpu as pltpu

F32 = jnp.float32
BF16 = jnp.bfloat16

D_MODEL = 1024
N_MIXERS = 2
ATTN_HEAD_DIM = 64
ATTN_HEADS = D_MODEL // ATTN_HEAD_DIM
DILATIONS = (1, 4, 16)
ATTN_BLOCK = 128
MLSTM_HEADS = 4
MLSTM_V_DIM = D_MODEL // MLSTM_HEADS
MLSTM_QK_DIM = MLSTM_V_DIM // 2
MLSTM_CHUNK = 256
CONV_WIDTH = 4
N_GROUPS = 8
EXPERTS_PER_GROUP = 8
N_EXPERTS = N_GROUPS * EXPERTS_PER_GROUP
EXPERT_FF = D_MODEL // 2
MOE_BLOCK = 256
NORM_EPS = 1e-6

LANES = 128
ROW_TILE = 256
VMEM_LIMIT = 48 * 1024 * 1024
NEG = -1e30


def _dot(a, b):
    return jnp.dot(a, b, preferred_element_type=F32)


def _dot_nt(a, b):
    return lax.dot_general(a, b, (((1,), (1,)), ((), ())), preferred_element_type=F32)


def _rms(x, gain):
    ms = jnp.mean(x * x, axis=-1, keepdims=True)
    return x * lax.rsqrt(ms + NORM_EPS) * gain


def _hi_lo(x):
    hi = x.astype(BF16)
    lo = (x - hi.astype(F32)).astype(BF16)
    return hi, lo


def _dot3(x, w_hi, w_lo):
    x_hi, x_lo = _hi_lo(x)
    return _dot(x_hi, w_hi) + (_dot(x_lo, w_hi) + _dot(x_hi, w_lo))


def _resident(shape):
    nd = len(shape)
    return pl.BlockSpec(shape, lambda *_: (0,) * nd, pipeline_mode=pl.Buffered(1))


def _params(*sem):
    return pltpu.CompilerParams(dimension_semantics=sem, vmem_limit_bytes=VMEM_LIMIT)


def _attn_in_kernel(x_ref, g_ref, w_ref, bd_ref, qkg_ref, o_ref):
    h = _rms(x_ref[...], g_ref[...]).astype(BF16)
    for c in range(3):
        cols = slice(c * D_MODEL, (c + 1) * D_MODEL)
        p = _dot(h, w_ref[:, cols])
        if c < 2:
            ss = _dot((p * p).astype(BF16), bd_ref[...])
            p = p * lax.rsqrt(ss * (1.0 / ATTN_HEAD_DIM) + NORM_EPS) * qkg_ref[:, cols]
        o_ref[:, cols] = p.astype(BF16)


def _attn_in_proj(x, gain, w, bd, qkg):
    n = x.shape[0]
    tm = ROW_TILE
    return pl.pallas_call(
        _attn_in_kernel,
        grid=(n // tm,),
        in_specs=[
            pl.BlockSpec((tm, D_MODEL), lambda i: (i, 0)),
            _resident((1, D_MODEL)),
            _resident((D_MODEL, 3 * D_MODEL)),
            _resident((D_MODEL, D_MODEL)),
            _resident((1, 2 * D_MODEL)),
        ],
        out_specs=pl.BlockSpec((tm, 3 * D_MODEL), lambda i: (i, 0)),
        out_shape=jax.ShapeDtypeStruct((n, 3 * D_MODEL), BF16),
        compiler_params=_params("parallel"),
        name="attn_in_proj",
    )(x, gain, w, bd, qkg)


def _attn_kernel(q_ref, kp_ref, kc_ref, vp_ref, vc_ref, o_ref, lse_ref):
    blk = ATTN_BLOCK
    has_prev = pl.program_id(2) > 0
    qi = lax.broadcasted_iota(jnp.int32, (blk, blk), 0)
    ki = lax.broadcasted_iota(jnp.int32, (blk, blk), 1)
    bias_p = jnp.where((ki >= qi) & has_prev, 0.0, NEG)
    bias_c = jnp.where(ki <= qi, 0.0, NEG)
    lane = lax.broadcasted_iota(jnp.int32, (1, LANES), 1)
    first = lane < ATTN_HEAD_DIM
    lse_acc = jnp.zeros((blk, LANES), F32)
    for pr in range(ATTN_HEADS // 2):
        cols = slice(pr * LANES, (pr + 1) * LANES)
        q2 = q_ref[:, cols]
        kp2, kc2 = kp_ref[:, cols], kc_ref[:, cols]
        vp2, vc2 = vp_ref[:, cols], vc_ref[:, cols]
        o_pair = None
        for hh in range(2):
            mine = first if hh == 0 else jnp.logical_not(first)
            qm = jnp.where(mine, q2, jnp.zeros_like(q2))
            sp = _dot_nt(qm, kp2) + bias_p
            sc = _dot_nt(qm, kc2) + bias_c
            m = jnp.maximum(jnp.max(sp, axis=-1, keepdims=True), jnp.max(sc, axis=-1, keepdims=True))
            pp = jnp.exp(sp - m)
            pc = jnp.exp(sc - m)
            l = jnp.sum(pp, axis=-1, keepdims=True) + jnp.sum(pc, axis=-1, keepdims=True)
            o2 = (_dot(pp.astype(BF16), vp2) + _dot(pc.astype(BF16), vc2)) * (1.0 / l)
            o_pair = o2 if hh == 0 else jnp.where(first, o_pair, o2)
            lse_acc = jnp.where(lane == 2 * pr + hh, m + jnp.log(l), lse_acc)
        o_ref[:, cols] = o_pair.astype(BF16)
    lse_ref[...] = lse_acc


def _attn_branch(qkv, batch, seq, dil):
    blk = ATTN_BLOCK
    rows = seq // dil
    nb = rows // blk
    qkv_v = qkv.reshape(batch, rows, dil * 3 * D_MODEL)

    def spec(c, prev):
        if prev:
            return pl.BlockSpec((None, blk, D_MODEL), lambda b, r, n: (b, jnp.maximum(n - 1, 0), 3 * r + c))
        return pl.BlockSpec((None, blk, D_MODEL), lambda b, r, n: (b, n, 3 * r + c))

    o, lse = pl.pallas_call(
        _attn_kernel,
        grid=(batch, dil, nb),
        in_specs=[spec(0, False), spec(1, True), spec(1, False), spec(2, True), spec(2, False)],
        out_specs=[
            pl.BlockSpec((None, blk, D_MODEL), lambda b, r, n: (b, n, r)),
            pl.BlockSpec((None, blk, LANES), lambda b, r, n: (b, n, r)),
        ],
        out_shape=[
            jax.ShapeDtypeStruct((batch, rows, dil * D_MODEL), BF16),
            jax.ShapeDtypeStruct((batch, rows, dil * LANES), F32),
        ],
        compiler_params=_params("parallel", "parallel", "arbitrary"),
        name=f"attn_dil{dil}",
    )(qkv_v, qkv_v, qkv_v, qkv_v, qkv_v)
    return o.reshape(batch * seq, D_MODEL), lse.reshape(batch * seq, LANES)


def _post_kernel(is_attn, *refs):
    if is_attn:
        o_refs, l_refs, e_ref = refs[0:3], refs[3:6], refs[6]
        refs = refs[7:]
    else:
        y_ref = refs[0]
        refs = refs[1:]
    (wout_ref, x_ref, gain_ref, wr_hi_ref, wr_lo_ref, rb_ref, tri_ref,
     xo_ref, h_ref, route_ref, cnt_ref, base_ref) = refs

    if is_attn:
        ls = [r[...] for r in l_refs]
        mx = jnp.maximum(jnp.maximum(ls[0], ls[1]), ls[2])
        es = [jnp.exp(l - mx) for l in ls]
        inv = 1.0 / (es[0] + es[1] + es[2])
        y = None
        for g in range(3):
            w_hi, w_lo = _hi_lo(es[g] * inv)
            w_full = _dot(w_hi, e_ref[...]) + _dot(w_lo, e_ref[...])
            t = w_full * o_refs[g][...].astype(F32)
            y = t if y is None else y + t
        y = y.astype(BF16)
    else:
        y = y_ref[...]

    xo = x_ref[...] + _dot(y, wout_ref[...])
    xo_ref[...] = xo
    hf = _rms(xo, gain_ref[...])
    h_ref[...] = hf

    logits = _dot3(hf, wr_hi_ref[...], wr_lo_ref[...]) + rb_ref[...]
    lane = lax.broadcasted_iota(jnp.int32, (1, LANES), 1)
    lanef = lane.astype(F32)
    is_grp = lane < N_GROUPS
    gl = jnp.where(is_grp, logits, NEG)
    gmax = jnp.max(gl, axis=-1, keepdims=True)
    gidx = jnp.min(jnp.where(gl == gmax, lanef, float(LANES)), axis=-1, keepdims=True)
    p_grp = 1.0 / jnp.sum(jnp.where(is_grp, jnp.exp(gl - gmax), 0.0), axis=-1, keepdims=True)
    lane_grp = ((lane - N_GROUPS) >> 3).astype(F32)
    in_grp = (lane >= N_GROUPS) & (lane < N_GROUPS + N_EXPERTS) & (lane_grp == gidx)
    em = jnp.where(in_grp, logits, NEG)
    v1 = jnp.max(em, axis=-1, keepdims=True)
    i1 = jnp.min(jnp.where(em == v1, lanef, float(LANES)), axis=-1, keepdims=True)
    hot0 = lanef == i1
    em2 = jnp.where(hot0, NEG, em)
    v2 = jnp.max(em2, axis=-1, keepdims=True)
    i2 = jnp.min(jnp.where(em2 == v2, lanef, float(LANES)), axis=-1, keepdims=True)
    hot1 = lanef == i2
    t2 = jnp.exp(v2 - v1)
    g0 = p_grp / (1.0 + t2)
    g1 = g0 * t2

    @pl.when(pl.program_id(0) == 0)
    def _():
        base_ref[...] = jnp.zeros_like(base_ref)

    hot = jnp.where(hot0 | hot1, 1.0, 0.0)
    before = _dot(tri_ref[...], hot.astype(BF16)) + base_ref[0:1, :]
    rank0 = jnp.sum(jnp.where(hot0, before, 0.0), axis=-1, keepdims=True)
    rank1 = jnp.sum(jnp.where(hot1, before, 0.0), axis=-1, keepdims=True)
    total = base_ref[0:1, :] + jnp.sum(hot, axis=0, keepdims=True)
    base_ref[...] = jnp.broadcast_to(total, base_ref.shape)
    cnt_ref[...] = jnp.broadcast_to(total, cnt_ref.shape)

    route = jnp.where(lane == 0, i1 - N_GROUPS, 0.0)
    route = jnp.where(lane == 1, i2 - N_GROUPS, route)
    route = jnp.where(lane == 2, rank0, route)
    route = jnp.where(lane == 3, rank1, route)
    route = jnp.where(lane == 4, g0, route)
    route = jnp.where(lane == 5, g1, route)
    route_ref[...] = route


def _post_mixer(mix, x, wout, gain, wr_hi, wr_lo, rb, expand=None):
    n = x.shape[0]
    tm = ROW_TILE
    is_attn = expand is not None
    row = lambda w: pl.BlockSpec((tm, w), lambda i: (i, 0))
    tri = (lax.broadcasted_iota(jnp.int32, (tm, tm), 0) > lax.broadcasted_iota(jnp.int32, (tm, tm), 1)).astype(BF16)
    if is_attn:
        mix_specs = [row(D_MODEL)] * 3 + [row(LANES)] * 3 + [_resident((LANES, D_MODEL))]
        mix_args = tuple(mix) + (expand,)
    else:
        mix_specs = [row(D_MODEL)]
        mix_args = tuple(mix)
    return pl.pallas_call(
        functools.partial(_post_kernel, is_attn),
        grid=(n // tm,),
        in_specs=mix_specs + [
            _resident((D_MODEL, D_MODEL)),
            row(D_MODEL),
            _resident((1, D_MODEL)),
            _resident((D_MODEL, LANES)),
            _resident((D_MODEL, LANES)),
            _resident((1, LANES)),
            _resident((tm, tm)),
        ],
        out_specs=[row(D_MODEL), row(D_MODEL), row(LANES), pl.BlockSpec((8, LANES), lambda i: (0, 0))],
        out_shape=[
            jax.ShapeDtypeStruct((n, D_MODEL), F32),
            jax.ShapeDtypeStruct((n, D_MODEL), F32),
            jax.ShapeDtypeStruct((n, LANES), F32),
            jax.ShapeDtypeStruct((8, LANES), F32),
        ],
        scratch_shapes=[pltpu.VMEM((8, LANES), F32)],
        compiler_params=_params("arbitrary"),
        name="post_attn" if is_attn else "post_mlstm",
    )(*mix_args, wout, x, gain, wr_hi, wr_lo, rb, tri)


def _dispatch_kernel(dest_ref, h_hbm, xs_init, xs_hbm, sem):
    del xs_init
    tm = ROW_TILE
    base = pl.program_id(0) * tm

    def copy(t, k):
        return pltpu.make_async_copy(h_hbm.at[pl.ds(base + t, 1)],
                                     xs_hbm.at[pl.ds(dest_ref[0, 0, 2 * t + k], 1)], sem)

    def start(t, c):
        copy(t, 0).start()
        copy(t, 1).start()
        return c

    def wait(t, c):
        copy(t, 0).wait()
        copy(t, 1).wait()
        return c

    lax.fori_loop(0, tm, start, 0)
    lax.fori_loop(0, tm, wait, 0)


def _dispatch(h, dest, p_rows):
    n = h.shape[0]
    tm = ROW_TILE
    return pl.pallas_call(
        _dispatch_kernel,
        grid=(n // tm,),
        in_specs=[
            pl.BlockSpec((1, 1, 2 * tm), lambda i: (i, 0, 0), memory_space=pltpu.SMEM),
            pl.BlockSpec(memory_space=pl.ANY),
            pl.BlockSpec(memory_space=pl.ANY),
        ],
        out_specs=pl.BlockSpec(memory_space=pl.ANY),
        out_shape=jax.ShapeDtypeStruct((p_rows, D_MODEL), F32),
        scratch_shapes=[pltpu.SemaphoreType.DMA(())],
        input_output_aliases={2: 0},
        compiler_params=_params("arbitrary"),
        name="moe_dispatch",
    )(dest.reshape(n // tm, 1, 2 * tm), h, jnp.zeros((p_rows, D_MODEL), F32))


def _expert_kernel(bexp_ref, nused_ref, x_ref, wg_ref, wu_ref, wd_ref, o_ref):
    del bexp_ref
    is_live = pl.program_id(0) < nused_ref[0]

    @pl.when(jnp.logical_not(is_live))
    def _():
        o_ref[...] = jnp.zeros_like(o_ref)

    @pl.when(is_live)
    def _():
        xb = x_ref[...].astype(BF16)
        g = _dot(xb, wg_ref[...].astype(BF16))
        u = _dot(xb, wu_ref[...].astype(BF16))
        a = (g * (1.0 / (1.0 + jnp.exp(-g))) * u).astype(BF16)
        o_ref[...] = _dot(a, wd_ref[...].astype(BF16))


def _experts(xs, block_exp, nused, w_gate, w_up, w_down):
    p_rows = xs.shape[0]
    nb = p_rows // MOE_BLOCK
    live = lambda i, be, nu: jnp.minimum(i, nu[0] - 1)
    wsel = lambda i, be, nu: (be[live(i, be, nu)], 0, 0)
    return pl.pallas_call(
        _expert_kernel,
        grid_spec=pltpu.PrefetchScalarGridSpec(
            num_scalar_prefetch=2,
            grid=(nb,),
            in_specs=[
                pl.BlockSpec((MOE_BLOCK, D_MODEL), lambda i, be, nu: (live(i, be, nu), 0)),
                pl.BlockSpec((None, D_MODEL, EXPERT_FF), wsel),
                pl.BlockSpec((None, D_MODEL, EXPERT_FF), wsel),
                pl.BlockSpec((None, EXPERT_FF, D_MODEL), wsel),
            ],
            out_specs=pl.BlockSpec((MOE_BLOCK, D_MODEL), lambda i, be, nu: (i, 0)),
        ),
        out_shape=jax.ShapeDtypeStruct((p_rows, D_MODEL), F32),
        compiler_params=_params("arbitrary"),
        name="moe_experts",
    )(block_exp, nused, xs, w_gate, w_up, w_down)


def _combine_kernel(dest_ref, ys_hbm, x_ref, route_ref, o_ref, buf, sem):
    tm = ROW_TILE

    def copy(t, k):
        return pltpu.make_async_copy(ys_hbm.at[pl.ds(dest_ref[0, 0, 2 * t + k], 1)],
                                     buf.at[k, pl.ds(t, 1)], sem)

    def start(t, c):
        copy(t, 0).start()
        copy(t, 1).start()
        return c

    def wait(t, c):
        copy(t, 0).wait()
        copy(t, 1).wait()
        return c

    lax.fori_loop(0, tm, start, 0)
    lax.fori_loop(0, tm, wait, 0)
    r = route_ref[...]
    o_ref[...] = x_ref[...] + r[:, 4:5] * buf[0] + r[:, 5:6] * buf[1]


def _combine(ys, dest, x, route):
    n = x.shape[0]
    tm = ROW_TILE
    return pl.pallas_call(
        _combine_kernel,
        grid=(n // tm,),
        in_specs=[
            pl.BlockSpec((1, 1, 2 * tm), lambda i: (i, 0, 0), memory_space=pltpu.SMEM),
            pl.BlockSpec(memory_space=pl.ANY),
            pl.BlockSpec((tm, D_MODEL), lambda i: (i, 0)),
            pl.BlockSpec((tm, LANES), lambda i: (i, 0)),
        ],
        out_specs=pl.BlockSpec((tm, D_MODEL), lambda i: (i, 0)),
        out_shape=jax.ShapeDtypeStruct((n, D_MODEL), F32),
        scratch_shapes=[pltpu.VMEM((2, tm, D_MODEL), F32), pltpu.SemaphoreType.DMA(())],
        compiler_params=_params("arbitrary"),
        name="moe_combine",
    )(dest.reshape(n // tm, 1, 2 * tm), ys, x, route)


def _moe(x_mid, h, route, counts, w_gate, w_up, w_down):
    n = x_mid.shape[0]
    nb = (n * 2) // MOE_BLOCK + N_EXPERTS
    cnt = counts[0, N_GROUPS:N_GROUPS + N_EXPERTS].astype(jnp.int32)
    padded = ((cnt + MOE_BLOCK - 1) // MOE_BLOCK) * MOE_BLOCK
    pad_end = jnp.cumsum(padded)
    pad_start = pad_end - padded
    ri = route[:, 0:4].astype(jnp.int32)
    dest = pad_start[ri[:, 0:2]] + ri[:, 2:4]
    block_exp = jnp.minimum(
        jnp.searchsorted(pad_end, jnp.arange(nb, dtype=jnp.int32) * MOE_BLOCK, side="right"),
        N_EXPERTS - 1).astype(jnp.int32)
    nused = (pad_end[-1:] // MOE_BLOCK).astype(jnp.int32)
    xs = _dispatch(h, dest, nb * MOE_BLOCK)
    ys = _experts(xs, block_exp, nused, w_gate, w_up, w_down)
    return _combine(ys, dest, x_mid, route)


def _mlstm_in_kernel(x_ref, g_ref, w_ref, wg_hi_ref, wg_lo_ref, gb_ref, qk_ref, v_ref, og_ref, gt_ref):
    hf = _rms(x_ref[...], g_ref[...])
    h = hf.astype(BF16)
    qk_ref[...] = _dot(h, w_ref[:, 0:D_MODEL])
    v_ref[...] = _dot(h, w_ref[:, D_MODEL:2 * D_MODEL]).astype(BF16)
    og_ref[...] = _dot(h, w_ref[:, 2 * D_MODEL:3 * D_MODEL]).astype(BF16)
    gt_ref[...] = _dot3(hf, wg_hi_ref[...], wg_lo_ref[...]) + gb_ref[...]


def _mlstm_in_proj(x, gain, w, wg_hi, wg_lo, gb):
    n = x.shape[0]
    tm = ROW_TILE
    row = lambda w_: pl.BlockSpec((tm, w_), lambda i: (i, 0))
    return pl.pallas_call(
        _mlstm_in_kernel,
        grid=(n // tm,),
        in_specs=[
            row(D_MODEL),
            _resident((1, D_MODEL)),
            _resident((D_MODEL, 3 * D_MODEL)),
            _resident((D_MODEL, LANES)),
            _resident((D_MODEL, LANES)),
            _resident((1, LANES)),
        ],
        out_specs=[row(D_MODEL), row(D_MODEL), row(D_MODEL), row(LANES)],
        out_shape=[
            jax.ShapeDtypeStruct((n, D_MODEL), F32),
            jax.ShapeDtypeStruct((n, D_MODEL), BF16),
            jax.ShapeDtypeStruct((n, D_MODEL), BF16),
            jax.ShapeDtypeStruct((n, LANES), F32),
        ],
        compiler_params=_params("parallel"),
        name="mlstm_in_proj",
    )(x, gain, w, wg_hi, wg_lo, gb)


def _log_sigmoid(x):
    return jnp.minimum(x, 0.0) - jnp.log(1.0 + jnp.exp(-jnp.abs(x)))


def _mlstm_kernel(qk_ref, v_ref, og_ref, gt_ref, cw_ref, cb_ref, hg_ref, tri_ref, y_ref,
                  cbuf, c_sc, n_sc, m_sc):
    L = MLSTM_CHUNK
    dk, dv = MLSTM_QK_DIM, MLSTM_V_DIM

    @pl.when(pl.program_id(1) == 0)
    def _():
        cbuf[0:8, :] = jnp.zeros((8, D_MODEL), F32)
        c_sc[...] = jnp.zeros_like(c_sc)
        n_sc[...] = jnp.zeros_like(n_sc)
        m_sc[...] = jnp.zeros_like(m_sc)

    cbuf[8:8 + L, :] = qk_ref[...]
    conv = cb_ref[...] + cw_ref[0:1, :] * cbuf[5:5 + L, :]
    for j in range(1, CONV_WIDTH):
        conv = conv + cw_ref[j:j + 1, :] * cbuf[5 + j:5 + j + L, :]
    cbuf[0:8, :] = cbuf[L:L + 8, :]
    qk = conv * (1.0 / (1.0 + jnp.exp(-conv)))

    gates = gt_ref[...]
    lf = _log_sigmoid(gates)
    lf_hi = lf.astype(BF16)
    lf_mid, lf_lo = _hi_lo(lf - lf_hi.astype(F32))
    tri = tri_ref[...]
    b = _dot(tri, lf_hi) + (_dot(tri, lf_mid) + _dot(tri, lf_lo))
    gates_t = gates.T
    b_t = b.T
    ti = lax.broadcasted_iota(jnp.int32, (L, L), 0)
    si = lax.broadcasted_iota(jnp.int32, (L, L), 1)
    causal = ti >= si

    for h in range(MLSTM_HEADS):
        i_col, i_row = gates[:, h:h + 1], gates_t[h:h + 1, :]
        b_col, b_row = b[:, 4 + h:5 + h], b_t[4 + h:5 + h, :]
        m_prev = m_sc[h, 0:1, 0:1]
        q = qk[:, h * dk:(h + 1) * dk]
        k = qk[:, D_MODEL // 2 + h * dk:D_MODEL // 2 + (h + 1) * dk] * (dk ** -0.5)
        qb, kb = q.astype(BF16), k.astype(BF16)
        v = v_ref[:, h * dv:(h + 1) * dv]

        dmat = jnp.where(causal, b_col - b_row + i_row, NEG)
        m_inter = b_col + m_prev
        m_t = jnp.maximum(m_inter, jnp.max(dmat, axis=-1, keepdims=True))
        s = _dot_nt(qb, kb) * jnp.exp(dmat - m_t)
        w_inter = jnp.exp(m_inter - m_t)
        num = _dot(s.astype(BF16), v) + w_inter * _dot(qb, c_sc[h].astype(BF16))
        den = jnp.sum(s, axis=-1, keepdims=True) + w_inter * jnp.sum(q * n_sc[h], axis=-1, keepdims=True)
        hh = num / jnp.maximum(jnp.abs(den), jnp.exp(-m_t))

        b_last = b_col[L - 1:L, :]
        g = b_last - b_col + i_col
        m_new = jnp.maximum(b_last + m_prev, jnp.max(g, axis=0, keepdims=True))
        decay = jnp.exp(b_last + m_prev - m_new)
        wk = jnp.exp(g - m_new) * k
        c_sc[h] = decay * c_sc[h] + _dot(wk.T.astype(BF16), v)
        n_sc[h] = decay * n_sc[h] + jnp.sum(wk, axis=0, keepdims=True)
        m_sc[h] = jnp.broadcast_to(m_new, (8, LANES))

        cols = slice(h * dv, (h + 1) * dv)
        hn = hh * lax.rsqrt(jnp.mean(hh * hh, axis=-1, keepdims=True) + NORM_EPS) * hg_ref[:, cols]
        og = og_ref[:, cols].astype(F32)
        y_ref[:, cols] = (hn * (1.0 / (1.0 + jnp.exp(-og)))).astype(BF16)


def _mlstm_cell(qk, v, og, gates, conv_w, conv_b, h_gain, batch, seq):
    L = MLSTM_CHUNK
    nc = seq // L
    row = lambda w_: pl.BlockSpec((L, w_), lambda b, c: (b * nc + c, 0))
    tri = (lax.broadcasted_iota(jnp.int32, (L, L), 0) >= lax.broadcasted_iota(jnp.int32, (L, L), 1)).astype(BF16)
    return pl.pallas_call(
        _mlstm_kernel,
        grid=(batch, nc),
        in_specs=[
            row(D_MODEL), row(D_MODEL), row(D_MODEL), row(LANES),
            _resident((CONV_WIDTH, D_MODEL)),
            _resident((1, D_MODEL)),
            _resident((1, D_MODEL)),
            _resident((L, L)),
        ],
        out_specs=row(D_MODEL),
        out_shape=jax.ShapeDtypeStruct((batch * seq, D_MODEL), BF16),
        scratch_shapes=[
            pltpu.VMEM((L + 8, D_MODEL), F32),
            pltpu.VMEM((MLSTM_HEADS, MLSTM_QK_DIM, MLSTM_V_DIM), F32),
            pltpu.VMEM((MLSTM_HEADS, 1, MLSTM_QK_DIM), F32),
            pltpu.VMEM((MLSTM_HEADS, 8, LANES), F32),
        ],
        compiler_params=_params("arbitrary", "arbitrary"),
        name="mlstm_cell",
    )(qk, v, og, gates, conv_w, conv_b, h_gain, tri)


def _pad_lanes(w):
    return jnp.pad(w, ((0, 0), (0, LANES - w.shape[1])))


def kernel(x, norm_mix, norm_ffn, attn_w_in, attn_q_norm, attn_k_norm, attn_w_out, mlstm_w_in, mlstm_conv_w, mlstm_conv_b, mlstm_gate_b, mlstm_h_norm, mlstm_w_out, moe_w_group, moe_b_group, moe_w_expert, moe_b_expert, moe_w_gate, moe_w_up, moe_w_down):
    batch, seq, d = x.shape
    assert d == D_MODEL and seq % (ATTN_BLOCK * max(DILATIONS)) == 0 and seq % MLSTM_CHUNK == 0
    assert (batch * seq) % ROW_TILE == 0
    depth = norm_mix.shape[0]
    n = batch * seq
    xf = x.reshape(n, d)

    head_of = jnp.arange(D_MODEL) // ATTN_HEAD_DIM
    bd = (head_of[:, None] == head_of[None, :]).astype(BF16)
    expand = (jnp.arange(LANES)[:, None] == head_of[None, :]).astype(BF16)

    for layer in range(depth):
        j = layer // N_MIXERS
        gain = norm_mix[layer].reshape(1, d)
        wr = jnp.concatenate([moe_w_group[layer], moe_w_expert[layer]], axis=1)
        wr_hi, wr_lo = _hi_lo(_pad_lanes(wr))
        rb = _pad_lanes(jnp.concatenate([moe_b_group[layer], moe_b_expert[layer]]).reshape(1, -1))
        ffn_gain = norm_ffn[layer].reshape(1, d)
        if layer % N_MIXERS == 0:
            qkg = jnp.concatenate([jnp.tile(attn_q_norm[j], ATTN_HEADS) * ATTN_HEAD_DIM ** -0.5,
                                   jnp.tile(attn_k_norm[j], ATTN_HEADS)]).reshape(1, 2 * d)
            qkv = _attn_in_proj(xf, gain, attn_w_in[j].astype(BF16), bd, qkg)
            outs = [_attn_branch(qkv, batch, seq, dil) for dil in DILATIONS]
            mix = tuple(o for o, _ in outs) + tuple(l for _, l in outs)
            x_mid, h, route, counts = _post_mixer(mix, xf, attn_w_out[j].astype(BF16), ffn_gain,
                                                  wr_hi, wr_lo, rb, expand=expand)
        else:
            w_in = mlstm_w_in[j]
            wg_hi, wg_lo = _hi_lo(_pad_lanes(w_in[:, 3 * d:]))
            gb = _pad_lanes(mlstm_gate_b[j].reshape(1, -1))
            qk, v, og, gates = _mlstm_in_proj(xf, gain, w_in[:, :3 * d].astype(BF16), wg_hi, wg_lo, gb)
            y = _mlstm_cell(qk, v, og, gates, mlstm_conv_w[j], mlstm_conv_b[j].reshape(1, d),
                            mlstm_h_norm[j].reshape(1, d), batch, seq)
            x_mid, h, route, counts = _post_mixer((y,), xf, mlstm_w_out[j].astype(BF16), ffn_gain,
                                                  wr_hi, wr_lo, rb)
        xf = _moe(x_mid, h, route, counts, moe_w_gate[layer], moe_w_up[layer], moe_w_down[layer])
    return xf.reshape(batch, seq, d)
```

```python
import functools

import jax
import jax.numpy as jnp
from jax import lax
from jax.experimental import pallas as pl
from jax.experimental.pallas import tpu as pltpu

F32 = jnp.float32
BF16 = jnp.bfloat16

D_MODEL = 1024
N_MIXERS = 2
ATTN_HEAD_DIM = 64
ATTN_HEADS = D_MODEL // ATTN_HEAD_DIM
DILATIONS = (1, 4, 16)
ATTN_BLOCK = 128
MLSTM_HEADS = 4
MLSTM_V_DIM = D_MODEL // MLSTM_HEADS
MLSTM_QK_DIM = MLSTM_V_DIM // 2
MLSTM_CHUNK = 256
CONV_WIDTH = 4
N_GROUPS = 8
EXPERTS_PER_GROUP = 8
N_EXPERTS = N_GROUPS * EXPERTS_PER_GROUP
EXPERT_FF = D_MODEL // 2
MOE_BLOCK = 256
NORM_EPS = 1e-6

LANES = 128
ROW_TILE = 256
GATHER_TILE = 512
VMEM_LIMIT = 48 * 1024 * 1024
NEG = -1e30


def _dot(a, b):
    return jnp.dot(a, b, preferred_element_type=F32)


def _dot_nt(a, b):
    return lax.dot_general(a, b, (((1,), (1,)), ((), ())), preferred_element_type=F32)


def _rms(x, gain):
    ms = jnp.mean(x * x, axis=-1, keepdims=True)
    return x * lax.rsqrt(ms + NORM_EPS) * gain


def _hi_lo(x):
    hi = x.astype(BF16)
    lo = (x - hi.astype(F32)).astype(BF16)
    return hi, lo


def _dot3(x, w_hi, w_lo):
    x_hi, x_lo = _hi_lo(x)
    return _dot(x_hi, w_hi) + (_dot(x_lo, w_hi) + _dot(x_hi, w_lo))


def _resident(shape):
    nd = len(shape)
    return pl.BlockSpec(shape, lambda *_: (0,) * nd, pipeline_mode=pl.Buffered(1))


def _params(*sem):
    return pltpu.CompilerParams(dimension_semantics=sem, vmem_limit_bytes=VMEM_LIMIT)


def _attn_in_kernel(x_ref, g_ref, w_ref, bd_ref, qkg_ref, *rest):
    o_refs, p_sc = rest[:len(DILATIONS)], rest[len(DILATIONS)]
    tm = x_ref.shape[0]
    h = _rms(x_ref[...], g_ref[...]).astype(BF16)
    for c in range(3):
        cols = slice(c * D_MODEL, (c + 1) * D_MODEL)
        p = _dot(h, w_ref[:, cols])
        if c < 2:
            ss = _dot((p * p).astype(BF16), bd_ref[...])
            p = p * lax.rsqrt(ss * (1.0 / ATTN_HEAD_DIM) + NORM_EPS) * qkg_ref[:, cols]
        for j in range(D_MODEL // LANES):
            p_sc[j] = p[:, j * LANES:(j + 1) * LANES]
        for o_ref, dil in zip(o_refs, DILATIONS):
            for r in range(dil):
                for j in range(D_MODEL // LANES):
                    dst = (3 * r + c) * D_MODEL + j * LANES
                    o_ref[:, dst:dst + LANES] = p_sc[j, pl.ds(r, tm // dil, stride=dil), :].astype(BF16)


def _attn_in_proj(x, gain, w, bd, qkg):
    n = x.shape[0]
    tm = ROW_TILE
    return pl.pallas_call(
        _attn_in_kernel,
        grid=(n // tm,),
        in_specs=[
            pl.BlockSpec((tm, D_MODEL), lambda i: (i, 0)),
            _resident((1, D_MODEL)),
            _resident((D_MODEL, 3 * D_MODEL)),
            _resident((D_MODEL, D_MODEL)),
            _resident((1, 2 * D_MODEL)),
        ],
        out_specs=[pl.BlockSpec((tm // dil, dil * 3 * D_MODEL), lambda i: (i, 0)) for dil in DILATIONS],
        out_shape=[jax.ShapeDtypeStruct((n // dil, dil * 3 * D_MODEL), BF16) for dil in DILATIONS],
        scratch_shapes=[pltpu.VMEM((D_MODEL // LANES, tm, LANES), F32)],
        compiler_params=_params("parallel"),
        name="attn_in_proj",
    )(x, gain, w, bd, qkg)


def _attn_kernel(q_ref, kp_ref, kc_ref, vp_ref, vc_ref, o_ref, lse_ref, s_sc, m_sc):
    blk = ATTN_BLOCK
    has_prev = pl.program_id(2) > 0
    qi = lax.broadcasted_iota(jnp.int32, (blk, blk), 0)
    ki = lax.broadcasted_iota(jnp.int32, (blk, blk), 1)
    bias = jnp.concatenate([jnp.where((ki >= qi) & has_prev, 0.0, NEG), jnp.where(ki <= qi, 0.0, NEG)], axis=1)
    lane = lax.broadcasted_iota(jnp.int32, (1, LANES), 1)
    first = lane < ATTN_HEAD_DIM
    ones = jnp.ones((2 * blk, LANES), BF16)

    for pr in range(ATTN_HEADS // 2):
        cols = slice(pr * LANES, (pr + 1) * LANES)
        q2 = q_ref[:, cols]
        k_band = jnp.concatenate([kp_ref[:, cols], kc_ref[:, cols]], axis=0)
        for hh in range(2):
            h = 2 * pr + hh
            mine = first if hh == 0 else jnp.logical_not(first)
            qm = jnp.where(mine, q2, jnp.zeros_like(q2))
            s = _dot_nt(qm, k_band) + bias
            s_sc[h] = s
            m_sc[h] = jnp.broadcast_to(jnp.max(jnp.maximum(s[:, :blk], s[:, blk:]), axis=-1, keepdims=True),
                                       (blk, LANES))

    m_acc = jnp.zeros((blk, LANES), F32)
    l_acc = jnp.ones((blk, LANES), F32)
    for pr in range(ATTN_HEADS // 2):
        cols = slice(pr * LANES, (pr + 1) * LANES)
        v_band = jnp.concatenate([vp_ref[:, cols], vc_ref[:, cols]], axis=0)
        v_ones = jnp.concatenate([v_band, ones], axis=1)
        o_pair = None
        for hh in range(2):
            h = 2 * pr + hh
            m = m_sc[h]
            p = jnp.concatenate([jnp.exp(s_sc[h, :, 0:blk] - m).astype(BF16),
                                 jnp.exp(s_sc[h, :, blk:2 * blk] - m).astype(BF16)], axis=1)
            ol = _dot(p, v_ones)
            l = ol[:, LANES:]
            o2 = ol[:, :LANES] * (1.0 / l)
            o_pair = o2 if hh == 0 else jnp.where(first, o_pair, o2)
            m_acc = jnp.where(lane == h, m, m_acc)
            l_acc = jnp.where(lane == h, l, l_acc)
        o_ref[:, cols] = o_pair.astype(BF16)
    lse_ref[...] = jnp.where(lane < ATTN_HEADS, m_acc + jnp.log(l_acc), 0.0)


def _attn_branch(qkv_d, batch, seq, dil):
    blk = ATTN_BLOCK
    rows = seq // dil
    nb = rows // blk
    qkv_v = qkv_d.reshape(batch, rows, dil * 3 * D_MODEL)

    def spec(c, prev):
        if prev:
            return pl.BlockSpec((None, blk, D_MODEL), lambda b, r, n: (b, jnp.maximum(n - 1, 0), 3 * r + c))
        return pl.BlockSpec((None, blk, D_MODEL), lambda b, r, n: (b, n, 3 * r + c))

    o, lse = pl.pallas_call(
        _attn_kernel,
        grid=(batch, dil, nb),
        in_specs=[spec(0, False), spec(1, True), spec(1, False), spec(2, True), spec(2, False)],
        out_specs=[
            pl.BlockSpec((None, blk, D_MODEL), lambda b, r, n: (b, n, r)),
            pl.BlockSpec((None, blk, LANES), lambda b, r, n: (b, n, r)),
        ],
        out_shape=[
            jax.ShapeDtypeStruct((batch, rows, dil * D_MODEL), BF16),
            jax.ShapeDtypeStruct((batch, rows, dil * LANES), F32),
        ],
        scratch_shapes=[
            pltpu.VMEM((ATTN_HEADS, blk, 2 * blk), F32),
            pltpu.VMEM((ATTN_HEADS, blk, LANES), F32),
        ],
        compiler_params=_params("parallel", "parallel", "arbitrary"),
        name=f"attn_dil{dil}",
    )(qkv_v, qkv_v, qkv_v, qkv_v, qkv_v)
    return o.reshape(batch * rows, dil * D_MODEL), lse.reshape(batch * rows, dil * LANES)


def _post_kernel(is_attn, *refs):
    if is_attn:
        o_refs, l_refs, e_ref = refs[0:3], refs[3:6], refs[6]
        refs = refs[7:]
    else:
        y_ref = refs[0]
        refs = refs[1:]
    (wout_ref, x_ref, gain_ref, wr_hi_ref, wr_lo_ref, rb_ref, tri_ref,
     xo_ref, h_ref, route_ref, cnt_ref, base_ref) = refs[:12]

    if is_attn:
        o_sc, l_sc = refs[12:]
        tm = x_ref.shape[0]
        nj = D_MODEL // LANES
        for g, dil in enumerate(DILATIONS):
            for r in range(dil):
                rows = pl.ds(r, tm // dil, stride=dil)
                for j in range(nj):
                    src = r * D_MODEL + j * LANES
                    o_sc[g * nj + j, rows, :] = o_refs[g][:, src:src + LANES].astype(F32)
                l_sc[g, rows, :] = l_refs[g][:, r * LANES:(r + 1) * LANES]
        ls = [l_sc[g] for g in range(3)]
        mx = jnp.maximum(jnp.maximum(ls[0], ls[1]), ls[2])
        es = [jnp.exp(l - mx) for l in ls]
        inv = 1.0 / (es[0] + es[1] + es[2])
        y = None
        for g in range(3):
            w_hi, w_lo = _hi_lo(es[g] * inv)
            w_full = _dot(w_hi, e_ref[...]) + _dot(w_lo, e_ref[...])
            o_g = jnp.concatenate([o_sc[g * nj + j] for j in range(nj)], axis=1)
            t = w_full * o_g
            y = t if y is None else y + t
        y = y.astype(BF16)
    else:
        y = y_ref[...]

    xo = x_ref[...] + _dot(y, wout_ref[...])
    xo_ref[...] = xo
    hf = _rms(xo, gain_ref[...])
    h_ref[...] = hf

    logits = _dot3(hf, wr_hi_ref[...], wr_lo_ref[...]) + rb_ref[...]
    lane = lax.broadcasted_iota(jnp.int32, (1, LANES), 1)
    lanef = lane.astype(F32)
    is_grp = lane < N_GROUPS
    gl = jnp.where(is_grp, logits, NEG)
    gmax = jnp.max(gl, axis=-1, keepdims=True)
    gidx = jnp.min(jnp.where(gl == gmax, lanef, float(LANES)), axis=-1, keepdims=True)
    p_grp = 1.0 / jnp.sum(jnp.where(is_grp, jnp.exp(gl - gmax), 0.0), axis=-1, keepdims=True)
    lane_grp = ((lane - N_GROUPS) >> 3).astype(F32)
    in_grp = (lane >= N_GROUPS) & (lane < N_GROUPS + N_EXPERTS) & (lane_grp == gidx)
    em = jnp.where(in_grp, logits, NEG)
    v1 = jnp.max(em, axis=-1, keepdims=True)
    i1 = jnp.min(jnp.where(em == v1, lanef, float(LANES)), axis=-1, keepdims=True)
    hot0 = lanef == i1
    em2 = jnp.where(hot0, NEG, em)
    v2 = jnp.max(em2, axis=-1, keepdims=True)
    i2 = jnp.min(jnp.where(em2 == v2, lanef, float(LANES)), axis=-1, keepdims=True)
    hot1 = lanef == i2
    t2 = jnp.exp(v2 - v1)
    g0 = p_grp / (1.0 + t2)
    g1 = g0 * t2

    @pl.when(pl.program_id(0) == 0)
    def _():
        base_ref[...] = jnp.zeros_like(base_ref)

    hot = jnp.where(hot0 | hot1, 1.0, 0.0)
    before = _dot(tri_ref[...], hot.astype(BF16)) + base_ref[0:1, :]
    rank0 = jnp.sum(jnp.where(hot0, before, 0.0), axis=-1, keepdims=True)
    rank1 = jnp.sum(jnp.where(hot1, before, 0.0), axis=-1, keepdims=True)
    total = base_ref[0:1, :] + jnp.sum(hot, axis=0, keepdims=True)
    base_ref[...] = jnp.broadcast_to(total, base_ref.shape)
    cnt_ref[...] = jnp.broadcast_to(total, cnt_ref.shape)

    route = jnp.where(lane == 0, i1 - N_GROUPS, 0.0)
    route = jnp.where(lane == 1, i2 - N_GROUPS, route)
    route = jnp.where(lane == 2, rank0, route)
    route = jnp.where(lane == 3, rank1, route)
    route = jnp.where(lane == 4, g0, route)
    route = jnp.where(lane == 5, g1, route)
    route_ref[...] = route


def _post_mixer(mix, x, wout, gain, wr_hi, wr_lo, rb, expand=None):
    n = x.shape[0]
    tm = ROW_TILE
    is_attn = expand is not None
    row = lambda w: pl.BlockSpec((tm, w), lambda i: (i, 0))
    tri = (lax.broadcasted_iota(jnp.int32, (tm, tm), 0) > lax.broadcasted_iota(jnp.int32, (tm, tm), 1)).astype(BF16)
    if is_attn:
        dil_rows = lambda w: [pl.BlockSpec((tm // dil, dil * w), lambda i: (i, 0)) for dil in DILATIONS]
        mix_specs = dil_rows(D_MODEL) + dil_rows(LANES) + [_resident((LANES, D_MODEL))]
        mix_args = tuple(mix) + (expand,)
        scratch = [pltpu.VMEM((3 * D_MODEL // LANES, tm, LANES), F32), pltpu.VMEM((3, tm, LANES), F32)]
    else:
        mix_specs = [row(D_MODEL)]
        mix_args = tuple(mix)
        scratch = []
    return pl.pallas_call(
        functools.partial(_post_kernel, is_attn),
        grid=(n // tm,),
        in_specs=mix_specs + [
            _resident((D_MODEL, D_MODEL)),
            row(D_MODEL),
            _resident((1, D_MODEL)),
            _resident((D_MODEL, LANES)),
            _resident((D_MODEL, LANES)),
            _resident((1, LANES)),
            _resident((tm, tm)),
        ],
        out_specs=[row(D_MODEL), row(D_MODEL), row(LANES), pl.BlockSpec((8, LANES), lambda i: (0, 0))],
        out_shape=[
            jax.ShapeDtypeStruct((n, D_MODEL), F32),
            jax.ShapeDtypeStruct((n, D_MODEL), F32),
            jax.ShapeDtypeStruct((n, LANES), F32),
            jax.ShapeDtypeStruct((8, LANES), F32),
        ],
        scratch_shapes=[pltpu.VMEM((8, LANES), F32)] + scratch,
        compiler_params=_params("arbitrary"),
        name="post_attn" if is_attn else "post_mlstm",
    )(*mix_args, wout, x, gain, wr_hi, wr_lo, rb, tri)


def _dispatch_kernel(dest_ref, h_ref, xs_init, xs_hbm, sem):
    del xs_init
    tm = GATHER_TILE

    def copy(t, k):
        return pltpu.make_async_copy(h_ref.at[pl.ds(t, 1)],
                                     xs_hbm.at[pl.ds(dest_ref[0, 0, 2 * t + k], 1)], sem)

    def start(t, c):
        copy(t, 0).start()
        copy(t, 1).start()
        return c

    def wait(t, c):
        copy(t, 0).wait()
        copy(t, 1).wait()
        return c

    lax.fori_loop(0, tm, start, 0)
    lax.fori_loop(0, tm, wait, 0)


def _dispatch(h, dest, p_rows):
    n = h.shape[0]
    tm = GATHER_TILE
    return pl.pallas_call(
        _dispatch_kernel,
        grid=(n // tm,),
        in_specs=[
            pl.BlockSpec((1, 1, 2 * tm), lambda i: (i, 0, 0), memory_space=pltpu.SMEM),
            pl.BlockSpec((tm, D_MODEL), lambda i: (i, 0)),
            pl.BlockSpec(memory_space=pl.ANY),
        ],
        out_specs=pl.BlockSpec(memory_space=pl.ANY),
        out_shape=jax.ShapeDtypeStruct((p_rows, D_MODEL), F32),
        scratch_shapes=[pltpu.SemaphoreType.DMA(())],
        input_output_aliases={2: 0},
        compiler_params=_params("arbitrary"),
        name="moe_dispatch",
    )(dest.reshape(n // tm, 1, 2 * tm), h, jnp.zeros((p_rows, D_MODEL), F32))


def _expert_kernel(bexp_ref, nused_ref, x_ref, wg_ref, wu_ref, wd_ref, o_ref):
    del bexp_ref
    is_live = pl.program_id(0) < nused_ref[0]

    @pl.when(jnp.logical_not(is_live))
    def _():
        o_ref[...] = jnp.zeros_like(o_ref)

    @pl.when(is_live)
    def _():
        xb = x_ref[...].astype(BF16)
        g = _dot(xb, wg_ref[...].astype(BF16))
        u = _dot(xb, wu_ref[...].astype(BF16))
        a = (g * (1.0 / (1.0 + jnp.exp(-g))) * u).astype(BF16)
        o_ref[...] = _dot(a, wd_ref[...].astype(BF16))


def _experts(xs, block_exp, nused, w_gate, w_up, w_down, layer):
    p_rows = xs.shape[0]
    nb = p_rows // MOE_BLOCK
    live = lambda i, be, nu: jnp.minimum(i, nu[0] - 1)
    wsel = lambda i, be, nu: (layer, be[live(i, be, nu)], 0, 0)
    return pl.pallas_call(
        _expert_kernel,
        grid_spec=pltpu.PrefetchScalarGridSpec(
            num_scalar_prefetch=2,
            grid=(nb,),
            in_specs=[
                pl.BlockSpec((MOE_BLOCK, D_MODEL), lambda i, be, nu: (live(i, be, nu), 0)),
                pl.BlockSpec((None, None, D_MODEL, EXPERT_FF), wsel),
                pl.BlockSpec((None, None, D_MODEL, EXPERT_FF), wsel),
                pl.BlockSpec((None, None, EXPERT_FF, D_MODEL), wsel),
            ],
            out_specs=pl.BlockSpec((MOE_BLOCK, D_MODEL), lambda i, be, nu: (i, 0)),
        ),
        out_shape=jax.ShapeDtypeStruct((p_rows, D_MODEL), F32),
        compiler_params=_params("arbitrary"),
        name="moe_experts",
    )(block_exp, nused, xs, w_gate, w_up, w_down)


def _combine_kernel(dest_ref, ys_hbm, x_ref, route_ref, o_ref, buf, sem):
    tm = GATHER_TILE

    def copy(t, k):
        return pltpu.make_async_copy(ys_hbm.at[pl.ds(dest_ref[0, 0, 2 * t + k], 1)],
                                     buf.at[k, pl.ds(t, 1)], sem)

    def start(t, c):
        copy(t, 0).start()
        copy(t, 1).start()
        return c

    def wait(t, c):
        copy(t, 0).wait()
        copy(t, 1).wait()
        return c

    lax.fori_loop(0, tm, start, 0)
    lax.fori_loop(0, tm, wait, 0)
    r = route_ref[...]
    o_ref[...] = x_ref[...] + r[:, 4:5] * buf[0] + r[:, 5:6] * buf[1]


def _combine(ys, dest, x, route):
    n = x.shape[0]
    tm = GATHER_TILE
    return pl.pallas_call(
        _combine_kernel,
        grid=(n // tm,),
        in_specs=[
            pl.BlockSpec((1, 1, 2 * tm), lambda i: (i, 0, 0), memory_space=pltpu.SMEM),
            pl.BlockSpec(memory_space=pl.ANY),
            pl.BlockSpec((tm, D_MODEL), lambda i: (i, 0)),
            pl.BlockSpec((tm, LANES), lambda i: (i, 0)),
        ],
        out_specs=pl.BlockSpec((tm, D_MODEL), lambda i: (i, 0)),
        out_shape=jax.ShapeDtypeStruct((n, D_MODEL), F32),
        scratch_shapes=[pltpu.VMEM((2, tm, D_MODEL), F32), pltpu.SemaphoreType.DMA(())],
        compiler_params=_params("arbitrary"),
        name="moe_combine",
    )(dest.reshape(n // tm, 1, 2 * tm), ys, x, route)


def _moe(x_mid, h, route, counts, w_gate, w_up, w_down, layer):
    n = x_mid.shape[0]
    nb = (n * 2) // MOE_BLOCK + N_EXPERTS
    cnt = counts[0, N_GROUPS:N_GROUPS + N_EXPERTS].astype(jnp.int32)
    padded = ((cnt + MOE_BLOCK - 1) // MOE_BLOCK) * MOE_BLOCK
    pad_end = jnp.cumsum(padded)
    pad_start = pad_end - padded
    ri = route[:, 0:4].astype(jnp.int32)
    dest = pad_start[ri[:, 0:2]] + ri[:, 2:4]
    block_row = jnp.arange(nb, dtype=jnp.int32) * MOE_BLOCK
    block_exp = jnp.minimum(jnp.sum(pad_end[None, :] <= block_row[:, None], axis=1), N_EXPERTS - 1).astype(jnp.int32)
    nused = (pad_end[-1:] // MOE_BLOCK).astype(jnp.int32)
    xs = _dispatch(h, dest, nb * MOE_BLOCK)
    ys = _experts(xs, block_exp, nused, w_gate, w_up, w_down, layer)
    return _combine(ys, dest, x_mid, route)


def _mlstm_in_kernel(x_ref, g_ref, w_ref, wg_hi_ref, wg_lo_ref, gb_ref, qk_ref, v_ref, og_ref, gt_ref):
    hf = _rms(x_ref[...], g_ref[...])
    h = hf.astype(BF16)
    qk_ref[...] = _dot(h, w_ref[:, 0:D_MODEL])
    v_ref[...] = _dot(h, w_ref[:, D_MODEL:2 * D_MODEL]).astype(BF16)
    og_ref[...] = _dot(h, w_ref[:, 2 * D_MODEL:3 * D_MODEL]).astype(BF16)
    gt_ref[...] = _dot3(hf, wg_hi_ref[...], wg_lo_ref[...]) + gb_ref[...]


def _mlstm_in_proj(x, gain, w, wg_hi, wg_lo, gb):
    n = x.shape[0]
    tm = ROW_TILE
    row = lambda w_: pl.BlockSpec((tm, w_), lambda i: (i, 0))
    return pl.pallas_call(
        _mlstm_in_kernel,
        grid=(n // tm,),
        in_specs=[
            row(D_MODEL),
            _resident((1, D_MODEL)),
            _resident((D_MODEL, 3 * D_MODEL)),
            _resident((D_MODEL, LANES)),
            _resident((D_MODEL, LANES)),
            _resident((1, LANES)),
        ],
        out_specs=[row(D_MODEL), row(D_MODEL), row(D_MODEL), row(LANES)],
        out_shape=[
            jax.ShapeDtypeStruct((n, D_MODEL), F32),
            jax.ShapeDtypeStruct((n, D_MODEL), BF16),
            jax.ShapeDtypeStruct((n, D_MODEL), BF16),
            jax.ShapeDtypeStruct((n, LANES), F32),
        ],
        compiler_params=_params("parallel"),
        name="mlstm_in_proj",
    )(x, gain, w, wg_hi, wg_lo, gb)


def _log_sigmoid(x):
    return jnp.minimum(x, 0.0) - jnp.log(1.0 + jnp.exp(-jnp.abs(x)))


def _mlstm_kernel(qk_ref, v_ref, og_ref, gt_ref, cw_ref, cb_ref, hg_ref, tri_ref, y_ref,
                  cbuf, c_sc, n_sc, m_sc):
    L = MLSTM_CHUNK
    dk, dv = MLSTM_QK_DIM, MLSTM_V_DIM

    @pl.when(pl.program_id(1) == 0)
    def _():
        cbuf[0:8, :] = jnp.zeros((8, D_MODEL), F32)
        c_sc[...] = jnp.zeros_like(c_sc)
        n_sc[...] = jnp.zeros_like(n_sc)
        m_sc[...] = jnp.zeros_like(m_sc)

    cbuf[8:8 + L, :] = qk_ref[...]
    conv = cb_ref[...] + cw_ref[0:1, :] * cbuf[5:5 + L, :]
    for j in range(1, CONV_WIDTH):
        conv = conv + cw_ref[j:j + 1, :] * cbuf[5 + j:5 + j + L, :]
    cbuf[0:8, :] = cbuf[L:L + 8, :]
    qk = conv * (1.0 / (1.0 + jnp.exp(-conv)))

    gates = gt_ref[...]
    lf = _log_sigmoid(gates)
    lf_hi = lf.astype(BF16)
    lf_mid, lf_lo = _hi_lo(lf - lf_hi.astype(F32))
    tri = tri_ref[...]
    b = _dot(tri, lf_hi) + (_dot(tri, lf_mid) + _dot(tri, lf_lo))
    gates_t = gates.T
    b_t = b.T
    ti = lax.broadcasted_iota(jnp.int32, (L, L), 0)
    si = lax.broadcasted_iota(jnp.int32, (L, L), 1)
    causal = ti >= si

    for h in range(MLSTM_HEADS):
        i_col, i_row = gates[:, h:h + 1], gates_t[h:h + 1, :]
        b_col, b_row = b[:, 4 + h:5 + h], b_t[4 + h:5 + h, :]
        m_prev = m_sc[h, 0:1, 0:1]
        q = qk[:, h * dk:(h + 1) * dk]
        k = qk[:, D_MODEL // 2 + h * dk:D_MODEL // 2 + (h + 1) * dk] * (dk ** -0.5)
        qb, kb = q.astype(BF16), k.astype(BF16)
        v = v_ref[:, h * dv:(h + 1) * dv]

        dmat = jnp.where(causal, b_col - b_row + i_row, NEG)
        m_inter = b_col + m_prev
        m_t = jnp.maximum(m_inter, jnp.max(dmat, axis=-1, keepdims=True))
        s = _dot_nt(qb, kb) * jnp.exp(dmat - m_t)
        w_inter = jnp.exp(m_inter - m_t)
        num = _dot(s.astype(BF16), v) + w_inter * _dot(qb, c_sc[h].astype(BF16))
        den = jnp.sum(s, axis=-1, keepdims=True) + w_inter * jnp.sum(q * n_sc[h], axis=-1, keepdims=True)
        hh = num / jnp.maximum(jnp.abs(den), jnp.exp(-m_t))

        b_last = b_col[L - 1:L, :]
        g = b_last - b_col + i_col
        m_new = jnp.maximum(b_last + m_prev, jnp.max(g, axis=0, keepdims=True))
        decay = jnp.exp(b_last + m_prev - m_new)
        wk = jnp.exp(g - m_new) * k
        c_sc[h] = decay * c_sc[h] + _dot(wk.T.astype(BF16), v)
        n_sc[h] = decay * n_sc[h] + jnp.sum(wk, axis=0, keepdims=True)
        m_sc[h] = jnp.broadcast_to(m_new, (8, LANES))

        cols = slice(h * dv, (h + 1) * dv)
        hn = hh * lax.rsqrt(jnp.mean(hh * hh, axis=-1, keepdims=True) + NORM_EPS) * hg_ref[:, cols]
        og = og_ref[:, cols].astype(F32)
        y_ref[:, cols] = (hn * (1.0 / (1.0 + jnp.exp(-og)))).astype(BF16)


def _mlstm_cell(qk, v, og, gates, conv_w, conv_b, h_gain, batch, seq):
    L = MLSTM_CHUNK
    nc = seq // L
    row = lambda w_: pl.BlockSpec((L, w_), lambda b, c: (b * nc + c, 0))
    tri = (lax.broadcasted_iota(jnp.int32, (L, L), 0) >= lax.broadcasted_iota(jnp.int32, (L, L), 1)).astype(BF16)
    return pl.pallas_call(
        _mlstm_kernel,
        grid=(batch, nc),
        in_specs=[
            row(D_MODEL), row(D_MODEL), row(D_MODEL), row(LANES),
            _resident((CONV_WIDTH, D_MODEL)),
            _resident((1, D_MODEL)),
            _resident((1, D_MODEL)),
            _resident((L, L)),
        ],
        out_specs=row(D_MODEL),
        out_shape=jax.ShapeDtypeStruct((batch * seq, D_MODEL), BF16),
        scratch_shapes=[
            pltpu.VMEM((L + 8, D_MODEL), F32),
            pltpu.VMEM((MLSTM_HEADS, MLSTM_QK_DIM, MLSTM_V_DIM), F32),
            pltpu.VMEM((MLSTM_HEADS, 1, MLSTM_QK_DIM), F32),
            pltpu.VMEM((MLSTM_HEADS, 8, LANES), F32),
        ],
        compiler_params=_params("arbitrary", "arbitrary"),
        name="mlstm_cell",
    )(qk, v, og, gates, conv_w, conv_b, h_gain, tri)


def _pad_lanes(w):
    return jnp.pad(w, ((0, 0), (0, LANES - w.shape[1])))


def kernel(x, norm_mix, norm_ffn, attn_w_in, attn_q_norm, attn_k_norm, attn_w_out, mlstm_w_in, mlstm_conv_w, mlstm_conv_b, mlstm_gate_b, mlstm_h_norm, mlstm_w_out, moe_w_group, moe_b_group, moe_w_expert, moe_b_expert, moe_w_gate, moe_w_up, moe_w_down):
    batch, seq, d = x.shape
    assert d == D_MODEL and seq % (ATTN_BLOCK * max(DILATIONS)) == 0 and seq % MLSTM_CHUNK == 0
    assert (batch * seq) % GATHER_TILE == 0 and GATHER_TILE % ROW_TILE == 0
    depth = norm_mix.shape[0]
    n = batch * seq
    xf = x.reshape(n, d)

    head_of = jnp.arange(D_MODEL) // ATTN_HEAD_DIM
    bd = (head_of[:, None] == head_of[None, :]).astype(BF16)
    expand = (jnp.arange(LANES)[:, None] == head_of[None, :]).astype(BF16)

    for layer in range(depth):
        j = layer // N_MIXERS
        gain = norm_mix[layer].reshape(1, d)
        wr = jnp.concatenate([moe_w_group[layer], moe_w_expert[layer]], axis=1)
        wr_hi, wr_lo = _hi_lo(_pad_lanes(wr))
        rb = _pad_lanes(jnp.concatenate([moe_b_group[layer], moe_b_expert[layer]]).reshape(1, -1))
        ffn_gain = norm_ffn[layer].reshape(1, d)
        if layer % N_MIXERS == 0:
            qkg = jnp.concatenate([jnp.tile(attn_q_norm[j], ATTN_HEADS) * ATTN_HEAD_DIM ** -0.5,
                                   jnp.tile(attn_k_norm[j], ATTN_HEADS)]).reshape(1, 2 * d)
            qkvs = _attn_in_proj(xf, gain, attn_w_in[j].astype(BF16), bd, qkg)
            outs = [_attn_branch(qkv_d, batch, seq, dil) for qkv_d, dil in zip(qkvs, DILATIONS)]
            mix = tuple(o for o, _ in outs) + tuple(l for _, l in outs)
            x_mid, h, route, counts = _post_mixer(mix, xf, attn_w_out[j].astype(BF16), ffn_gain,
                                                  wr_hi, wr_lo, rb, expand=expand)
        else:
            w_in = mlstm_w_in[j]
            wg_hi, wg_lo = _hi_lo(_pad_lanes(w_in[:, 3 * d:]))
            gb = _pad_lanes(mlstm_gate_b[j].reshape(1, -1))
            qk, v, og, gates = _mlstm_in_proj(xf, gain, w_in[:, :3 * d].astype(BF16), wg_hi, wg_lo, gb)
            y = _mlstm_cell(qk, v, og, gates, mlstm_conv_w[j], mlstm_conv_b[j].reshape(1, d),
                            mlstm_h_norm[j].reshape(1, d), batch, seq)
            x_mid, h, route, counts = _post_mixer((y,), xf, mlstm_w_out[j].astype(BF16), ffn_gain,
                                                  wr_hi, wr_lo, rb)
        xf = _moe(x_mid, h, route, counts, moe_w_gate, moe_w_up, moe_w_down, layer)
    return xf.reshape(batch, seq, d)
```

```python
import functools

import jax
import jax.numpy as jnp
from jax import lax
from jax.experimental import pallas as pl
from jax.experimental.pallas import tpu as pltpu

F32 = jnp.float32
BF16 = jnp.bfloat16

D_MODEL = 1024
N_MIXERS = 2
ATTN_HEAD_DIM = 64
ATTN_HEADS = D_MODEL // ATTN_HEAD_DIM
DILATIONS = (1, 4, 16)
ATTN_BLOCK = 128
MLSTM_HEADS = 4
MLSTM_V_DIM = D_MODEL // MLSTM_HEADS
MLSTM_QK_DIM = MLSTM_V_DIM // 2
MLSTM_CHUNK = 256
CONV_WIDTH = 4
N_GROUPS = 8
EXPERTS_PER_GROUP = 8
N_EXPERTS = N_GROUPS * EXPERTS_PER_GROUP
EXPERT_FF = D_MODEL // 2
MOE_BLOCK = 256
NORM_EPS = 1e-6

LANES = 128
ROW_TILE = 256
GATHER_TILE = 512
VMEM_LIMIT = 48 * 1024 * 1024
NEG = -1e30


def _dot(a, b):
    return jnp.dot(a, b, preferred_element_type=F32)


def _dot_nt(a, b):
    return lax.dot_general(a, b, (((1,), (1,)), ((), ())), preferred_element_type=F32)


def _rms(x, gain):
    ms = jnp.mean(x * x, axis=-1, keepdims=True)
    return x * lax.rsqrt(ms + NORM_EPS) * gain


def _hi_lo(x):
    hi = x.astype(BF16)
    lo = (x - hi.astype(F32)).astype(BF16)
    return hi, lo


def _dot3(x, w_hi, w_lo):
    x_hi, x_lo = _hi_lo(x)
    return _dot(x_hi, w_hi) + (_dot(x_lo, w_hi) + _dot(x_hi, w_lo))


ROW_SLABS = D_MODEL // LANES


def _store_rows(ref, x):
    rows = x.shape[0]
    for j in range(ROW_SLABS):
        ref[pl.ds(j, rows, stride=ROW_SLABS), :] = x[:, j * LANES:(j + 1) * LANES]


def _load_rows(ref, rows):
    return jnp.concatenate([ref[pl.ds(j, rows, stride=ROW_SLABS), :] for j in range(ROW_SLABS)], axis=1)


def _resident(shape):
    nd = len(shape)
    return pl.BlockSpec(shape, lambda *_: (0,) * nd, pipeline_mode=pl.Buffered(1))


def _params(*sem):
    return pltpu.CompilerParams(dimension_semantics=sem, vmem_limit_bytes=VMEM_LIMIT)


def _dilation_perm(tm, dil):
    i = jnp.arange(tm)
    src = (i % (tm // dil)) * dil + i // (tm // dil)
    return (src[:, None] == jnp.arange(tm)[None, :]).astype(BF16)


def _attn_in_kernel(x_ref, g_ref, w_ref, bd_ref, qkg_ref, *rest):
    nd = len(DILATIONS)
    perm_refs, o_refs = rest[:nd - 1], rest[nd - 1:]
    tm = x_ref.shape[0]
    h = _rms(x_ref[...], g_ref[...]).astype(BF16)
    for c in range(3):
        cols = slice(c * D_MODEL, (c + 1) * D_MODEL)
        p = _dot(h, w_ref[:, cols])
        if c < 2:
            ss = _dot((p * p).astype(BF16), bd_ref[...])
            p = p * lax.rsqrt(ss * (1.0 / ATTN_HEAD_DIM) + NORM_EPS) * qkg_ref[:, cols]
        pb = p.astype(BF16)
        o_refs[0][:, cols] = pb
        for o_ref, perm_ref, dil in zip(o_refs[1:], perm_refs, DILATIONS[1:]):
            pp = _dot(perm_ref[...], pb).astype(BF16)
            for r in range(dil):
                dst = (3 * r + c) * D_MODEL
                o_ref[:, dst:dst + D_MODEL] = pp[r * (tm // dil):(r + 1) * (tm // dil), :]


def _attn_in_proj(x, gain, w, bd, qkg):
    n = x.shape[0]
    tm = ROW_TILE
    assert DILATIONS[0] == 1
    perms = [_dilation_perm(tm, dil) for dil in DILATIONS[1:]]
    return pl.pallas_call(
        _attn_in_kernel,
        grid=(n // tm,),
        in_specs=[
            pl.BlockSpec((tm, D_MODEL), lambda i: (i, 0)),
            _resident((1, D_MODEL)),
            _resident((D_MODEL, 3 * D_MODEL)),
            _resident((D_MODEL, D_MODEL)),
            _resident((1, 2 * D_MODEL)),
        ] + [_resident((tm, tm))] * len(perms),
        out_specs=[pl.BlockSpec((tm // dil, dil * 3 * D_MODEL), lambda i: (i, 0)) for dil in DILATIONS],
        out_shape=[jax.ShapeDtypeStruct((n // dil, dil * 3 * D_MODEL), BF16) for dil in DILATIONS],
        compiler_params=_params("parallel"),
        name="attn_in_proj",
    )(x, gain, w, bd, qkg, *perms)


def _attn_kernel(q_ref, kp_ref, kc_ref, vp_ref, vc_ref, o_ref, lse_ref, s_sc, m_sc):
    blk = ATTN_BLOCK
    has_prev = pl.program_id(2) > 0
    qi = lax.broadcasted_iota(jnp.int32, (blk, blk), 0)
    ki = lax.broadcasted_iota(jnp.int32, (blk, blk), 1)
    bias = jnp.concatenate([jnp.where((ki >= qi) & has_prev, 0.0, NEG), jnp.where(ki <= qi, 0.0, NEG)], axis=1)
    lane = lax.broadcasted_iota(jnp.int32, (1, LANES), 1)
    first = lane < ATTN_HEAD_DIM
    ones = jnp.ones((2 * blk, LANES), BF16)

    for pr in range(ATTN_HEADS // 2):
        cols = slice(pr * LANES, (pr + 1) * LANES)
        q2 = q_ref[:, cols]
        k_band = jnp.concatenate([kp_ref[:, cols], kc_ref[:, cols]], axis=0)
        for hh in range(2):
            h = 2 * pr + hh
            mine = first if hh == 0 else jnp.logical_not(first)
            qm = jnp.where(mine, q2, jnp.zeros_like(q2))
            s = _dot_nt(qm, k_band) + bias
            s_sc[h] = s
            m_sc[h] = jnp.broadcast_to(jnp.max(jnp.maximum(s[:, :blk], s[:, blk:]), axis=-1, keepdims=True),
                                       (blk, LANES))

    m_acc = jnp.zeros((blk, LANES), F32)
    l_acc = jnp.ones((blk, LANES), F32)
    for pr in range(ATTN_HEADS // 2):
        cols = slice(pr * LANES, (pr + 1) * LANES)
        v_band = jnp.concatenate([vp_ref[:, cols], vc_ref[:, cols]], axis=0)
        v_ones = jnp.concatenate([v_band, ones], axis=1)
        o_pair = None
        for hh in range(2):
            h = 2 * pr + hh
            m = m_sc[h]
            p = jnp.concatenate([jnp.exp(s_sc[h, :, 0:blk] - m).astype(BF16),
                                 jnp.exp(s_sc[h, :, blk:2 * blk] - m).astype(BF16)], axis=1)
            ol = _dot(p, v_ones)
            l = ol[:, LANES:]
            o2 = ol[:, :LANES] * (1.0 / l)
            o_pair = o2 if hh == 0 else jnp.where(first, o_pair, o2)
            m_acc = jnp.where(lane == h, m, m_acc)
            l_acc = jnp.where(lane == h, l, l_acc)
        o_ref[:, cols] = o_pair.astype(BF16)
    lse_ref[...] = jnp.where(lane < ATTN_HEADS, m_acc + jnp.log(l_acc), 0.0)


def _attn_branch(qkv_d, batch, seq, dil):
    blk = ATTN_BLOCK
    rows = seq // dil
    nb = rows // blk
    qkv_v = qkv_d.reshape(batch, rows, dil * 3 * D_MODEL)

    def spec(c, prev):
        if prev:
            return pl.BlockSpec((None, blk, D_MODEL), lambda b, r, n: (b, jnp.maximum(n - 1, 0), 3 * r + c))
        return pl.BlockSpec((None, blk, D_MODEL), lambda b, r, n: (b, n, 3 * r + c))

    o, lse = pl.pallas_call(
        _attn_kernel,
        grid=(batch, dil, nb),
        in_specs=[spec(0, False), spec(1, True), spec(1, False), spec(2, True), spec(2, False)],
        out_specs=[
            pl.BlockSpec((None, blk, D_MODEL), lambda b, r, n: (b, n, r)),
            pl.BlockSpec((None, blk, LANES), lambda b, r, n: (b, n, r)),
        ],
        out_shape=[
            jax.ShapeDtypeStruct((batch, rows, dil * D_MODEL), BF16),
            jax.ShapeDtypeStruct((batch, rows, dil * LANES), F32),
        ],
        scratch_shapes=[
            pltpu.VMEM((ATTN_HEADS, blk, 2 * blk), F32),
            pltpu.VMEM((ATTN_HEADS, blk, LANES), F32),
        ],
        compiler_params=_params("parallel", "parallel", "arbitrary"),
        name=f"attn_dil{dil}",
    )(qkv_v, qkv_v, qkv_v, qkv_v, qkv_v)
    return o.reshape(batch * rows, dil * D_MODEL), lse.reshape(batch * rows, dil * LANES)


def _post_kernel(is_attn, *refs):
    if is_attn:
        o_refs, l_refs, e_ref, unperm_refs = refs[0:3], refs[3:6], refs[6], refs[7:9]
        refs = refs[9:]
    else:
        y_ref = refs[0]
        refs = refs[1:]
    (wout_ref, x_ref, gain_ref, wr_hi_ref, wr_lo_ref, rb_ref, tri_ref,
     xo_ref, h_ref, route_ref, cnt_ref, base_ref) = refs[:12]

    if is_attn:
        l_sc = refs[12]
        tm = x_ref.shape[0]
        for g, dil in enumerate(DILATIONS):
            for r in range(dil):
                l_sc[g, pl.ds(r, tm // dil, stride=dil), :] = l_refs[g][:, r * LANES:(r + 1) * LANES]
        ls = [l_sc[g] for g in range(3)]
        mx = jnp.maximum(jnp.maximum(ls[0], ls[1]), ls[2])
        es = [jnp.exp(l - mx) for l in ls]
        inv = 1.0 / (es[0] + es[1] + es[2])
        y = None
        for g, dil in enumerate(DILATIONS):
            w_hi, w_lo = _hi_lo(es[g] * inv)
            w_full = _dot(w_hi, e_ref[...]) + _dot(w_lo, e_ref[...])
            if dil == 1:
                o_g = o_refs[g][...].astype(F32)
            else:
                o_rm = jnp.concatenate([o_refs[g][:, r * D_MODEL:(r + 1) * D_MODEL] for r in range(dil)], axis=0)
                o_g = _dot(unperm_refs[g - 1][...], o_rm)
            t = w_full * o_g
            y = t if y is None else y + t
        y = y.astype(BF16)
    else:
        y = y_ref[...]

    xo = x_ref[...] + _dot(y, wout_ref[...])
    xo_ref[...] = xo
    hf = _rms(xo, gain_ref[...])
    _store_rows(h_ref, hf)

    logits = _dot3(hf, wr_hi_ref[...], wr_lo_ref[...]) + rb_ref[...]
    lane = lax.broadcasted_iota(jnp.int32, (1, LANES), 1)
    lanef = lane.astype(F32)
    is_grp = lane < N_GROUPS
    gl = jnp.where(is_grp, logits, NEG)
    gmax = jnp.max(gl, axis=-1, keepdims=True)
    gidx = jnp.min(jnp.where(gl == gmax, lanef, float(LANES)), axis=-1, keepdims=True)
    p_grp = 1.0 / jnp.sum(jnp.where(is_grp, jnp.exp(gl - gmax), 0.0), axis=-1, keepdims=True)
    lane_grp = ((lane - N_GROUPS) >> 3).astype(F32)
    in_grp = (lane >= N_GROUPS) & (lane < N_GROUPS + N_EXPERTS) & (lane_grp == gidx)
    em = jnp.where(in_grp, logits, NEG)
    v1 = jnp.max(em, axis=-1, keepdims=True)
    i1 = jnp.min(jnp.where(em == v1, lanef, float(LANES)), axis=-1, keepdims=True)
    hot0 = lanef == i1
    em2 = jnp.where(hot0, NEG, em)
    v2 = jnp.max(em2, axis=-1, keepdims=True)
    i2 = jnp.min(jnp.where(em2 == v2, lanef, float(LANES)), axis=-1, keepdims=True)
    hot1 = lanef == i2
    t2 = jnp.exp(v2 - v1)
    g0 = p_grp / (1.0 + t2)
    g1 = g0 * t2

    @pl.when(pl.program_id(0) == 0)
    def _():
        base_ref[...] = jnp.zeros_like(base_ref)

    hot = jnp.where(hot0 | hot1, 1.0, 0.0)
    before = _dot(tri_ref[...], hot.astype(BF16)) + base_ref[0:1, :]
    rank0 = jnp.sum(jnp.where(hot0, before, 0.0), axis=-1, keepdims=True)
    rank1 = jnp.sum(jnp.where(hot1, before, 0.0), axis=-1, keepdims=True)
    total = base_ref[0:1, :] + jnp.sum(hot, axis=0, keepdims=True)
    base_ref[...] = jnp.broadcast_to(total, base_ref.shape)
    cnt_ref[...] = jnp.broadcast_to(total, cnt_ref.shape)

    route = jnp.where(lane == 0, i1 - N_GROUPS, 0.0)
    route = jnp.where(lane == 1, i2 - N_GROUPS, route)
    route = jnp.where(lane == 2, rank0, route)
    route = jnp.where(lane == 3, rank1, route)
    route = jnp.where(lane == 4, g0, route)
    route = jnp.where(lane == 5, g1, route)
    route_ref[...] = route


def _post_mixer(mix, x, wout, gain, wr_hi, wr_lo, rb, expand=None):
    n = x.shape[0]
    tm = ROW_TILE
    is_attn = expand is not None
    row = lambda w: pl.BlockSpec((tm, w), lambda i: (i, 0))
    tri = (lax.broadcasted_iota(jnp.int32, (tm, tm), 0) > lax.broadcasted_iota(jnp.int32, (tm, tm), 1)).astype(BF16)
    if is_attn:
        dil_rows = lambda w: [pl.BlockSpec((tm // dil, dil * w), lambda i: (i, 0)) for dil in DILATIONS]
        unperms = [_dilation_perm(tm, dil).T for dil in DILATIONS[1:]]
        mix_specs = (dil_rows(D_MODEL) + dil_rows(LANES) + [_resident((LANES, D_MODEL))]
                     + [_resident((tm, tm))] * len(unperms))
        mix_args = tuple(mix) + (expand,) + tuple(unperms)
        scratch = [pltpu.VMEM((3, tm, LANES), F32)]
    else:
        mix_specs = [row(D_MODEL)]
        mix_args = tuple(mix)
        scratch = []
    return pl.pallas_call(
        functools.partial(_post_kernel, is_attn),
        grid=(n // tm,),
        in_specs=mix_specs + [
            _resident((D_MODEL, D_MODEL)),
            row(D_MODEL),
            _resident((1, D_MODEL)),
            _resident((D_MODEL, LANES)),
            _resident((D_MODEL, LANES)),
            _resident((1, LANES)),
            _resident((tm, tm)),
        ],
        out_specs=[row(D_MODEL), pl.BlockSpec((tm * ROW_SLABS, LANES), lambda i: (i, 0)), row(LANES),
                   pl.BlockSpec((8, LANES), lambda i: (0, 0))],
        out_shape=[
            jax.ShapeDtypeStruct((n, D_MODEL), F32),
            jax.ShapeDtypeStruct((n * ROW_SLABS, LANES), F32),
            jax.ShapeDtypeStruct((n, LANES), F32),
            jax.ShapeDtypeStruct((8, LANES), F32),
        ],
        scratch_shapes=[pltpu.VMEM((8, LANES), F32)] + scratch,
        compiler_params=_params("arbitrary"),
        name="post_attn" if is_attn else "post_mlstm",
    )(*mix_args, wout, x, gain, wr_hi, wr_lo, rb, tri)


def _dispatch_kernel(dest_ref, h_ref, xs_init, xs_hbm, sem):
    del xs_init
    tm = GATHER_TILE

    def copy(t, k):
        src = pl.multiple_of(t * ROW_SLABS, ROW_SLABS)
        dst = pl.multiple_of(dest_ref[0, 0, 2 * t + k], ROW_SLABS)
        return pltpu.make_async_copy(h_ref.at[pl.ds(src, ROW_SLABS)], xs_hbm.at[pl.ds(dst, ROW_SLABS)], sem)

    def start(t, c):
        copy(t, 0).start()
        copy(t, 1).start()
        return c

    def wait(t, c):
        copy(t, 0).wait()
        copy(t, 1).wait()
        return c

    lax.fori_loop(0, tm, start, 0, unroll=4)
    lax.fori_loop(0, tm, wait, 0, unroll=4)


def _dispatch(h, dest, p_rows):
    n = h.shape[0] // ROW_SLABS
    tm = GATHER_TILE
    return pl.pallas_call(
        _dispatch_kernel,
        grid=(n // tm,),
        in_specs=[
            pl.BlockSpec((1, 1, 2 * tm), lambda i: (i, 0, 0), memory_space=pltpu.SMEM),
            pl.BlockSpec((tm * ROW_SLABS, LANES), lambda i: (i, 0)),
            pl.BlockSpec(memory_space=pl.ANY),
        ],
        out_specs=pl.BlockSpec(memory_space=pl.ANY),
        out_shape=jax.ShapeDtypeStruct((p_rows * ROW_SLABS, LANES), F32),
        scratch_shapes=[pltpu.SemaphoreType.DMA(())],
        input_output_aliases={2: 0},
        compiler_params=_params("arbitrary"),
        name="moe_dispatch",
    )(dest.reshape(n // tm, 1, 2 * tm), h, jnp.zeros((p_rows * ROW_SLABS, LANES), F32))


def _expert_kernel(bexp_ref, nused_ref, x_ref, wg_ref, wu_ref, wd_ref, o_ref):
    del bexp_ref
    is_live = pl.program_id(0) < nused_ref[0]

    @pl.when(jnp.logical_not(is_live))
    def _():
        o_ref[...] = jnp.zeros_like(o_ref)

    @pl.when(is_live)
    def _():
        xb = _load_rows(x_ref, MOE_BLOCK).astype(BF16)
        g = _dot(xb, wg_ref[...].astype(BF16))
        u = _dot(xb, wu_ref[...].astype(BF16))
        a = (g * (1.0 / (1.0 + jnp.exp(-g))) * u).astype(BF16)
        _store_rows(o_ref, _dot(a, wd_ref[...].astype(BF16)))


def _experts(xs, block_exp, nused, w_gate, w_up, w_down, layer):
    nb = xs.shape[0] // (MOE_BLOCK * ROW_SLABS)
    live = lambda i, be, nu: jnp.minimum(i, nu[0] - 1)
    wsel = lambda i, be, nu: (layer, be[live(i, be, nu)], 0, 0)
    return pl.pallas_call(
        _expert_kernel,
        grid_spec=pltpu.PrefetchScalarGridSpec(
            num_scalar_prefetch=2,
            grid=(nb,),
            in_specs=[
                pl.BlockSpec((MOE_BLOCK * ROW_SLABS, LANES), lambda i, be, nu: (live(i, be, nu), 0)),
                pl.BlockSpec((None, None, D_MODEL, EXPERT_FF), wsel),
                pl.BlockSpec((None, None, D_MODEL, EXPERT_FF), wsel),
                pl.BlockSpec((None, None, EXPERT_FF, D_MODEL), wsel),
            ],
            out_specs=pl.BlockSpec((MOE_BLOCK * ROW_SLABS, LANES), lambda i, be, nu: (i, 0)),
        ),
        out_shape=jax.ShapeDtypeStruct(xs.shape, F32),
        compiler_params=_params("arbitrary"),
        name="moe_experts",
    )(block_exp, nused, xs, w_gate, w_up, w_down)


def _combine_kernel(dcur_ref, dnext_ref, ys_hbm, x_ref, route_ref, o_ref, buf, sems):
    tm = GATHER_TILE
    i = pl.program_id(0)
    slot = i % 2

    def copy(d_ref, s, t, k):
        src = pl.multiple_of(d_ref[0, 0, 2 * t + k], ROW_SLABS)
        dst = pl.multiple_of(t * ROW_SLABS, ROW_SLABS)
        return pltpu.make_async_copy(ys_hbm.at[pl.ds(src, ROW_SLABS)],
                                     buf.at[s, k, pl.ds(dst, ROW_SLABS)], sems.at[s])

    def gather(d_ref, s):
        def start(t, c):
            copy(d_ref, s, t, 0).start()
            copy(d_ref, s, t, 1).start()
            return c
        lax.fori_loop(0, tm, start, 0, unroll=4)

    @pl.when(i == 0)
    def _():
        gather(dcur_ref, 0)

    @pl.when(i + 1 < pl.num_programs(0))
    def _():
        gather(dnext_ref, 1 - slot)

    def wait(t, c):
        copy(dcur_ref, slot, t, 0).wait()
        copy(dcur_ref, slot, t, 1).wait()
        return c

    lax.fori_loop(0, tm, wait, 0, unroll=4)
    r = route_ref[...]
    y0 = _load_rows(buf.at[slot, 0], tm)
    y1 = _load_rows(buf.at[slot, 1], tm)
    o_ref[...] = x_ref[...] + r[:, 4:5] * y0 + r[:, 5:6] * y1


def _combine(ys, dest, x, route):
    n = x.shape[0]
    tm = GATHER_TILE
    nt = n // tm
    dest_t = dest.reshape(nt, 1, 2 * tm)
    return pl.pallas_call(
        _combine_kernel,
        grid=(nt,),
        in_specs=[
            pl.BlockSpec((1, 1, 2 * tm), lambda i: (i, 0, 0), memory_space=pltpu.SMEM),
            pl.BlockSpec((1, 1, 2 * tm), lambda i: (jnp.minimum(i + 1, nt - 1), 0, 0), memory_space=pltpu.SMEM),
            pl.BlockSpec(memory_space=pl.ANY),
            pl.BlockSpec((tm, D_MODEL), lambda i: (i, 0)),
            pl.BlockSpec((tm, LANES), lambda i: (i, 0)),
        ],
        out_specs=pl.BlockSpec((tm, D_MODEL), lambda i: (i, 0)),
        out_shape=jax.ShapeDtypeStruct((n, D_MODEL), F32),
        scratch_shapes=[pltpu.VMEM((2, 2, tm * ROW_SLABS, LANES), F32), pltpu.SemaphoreType.DMA((2,))],
        compiler_params=_params("arbitrary"),
        name="moe_combine",
    )(dest_t, dest_t, ys, x, route)


def _moe(x_mid, h, route, counts, w_gate, w_up, w_down, layer):
    n = x_mid.shape[0]
    nb = (n * 2) // MOE_BLOCK + N_EXPERTS
    cnt = counts[0, N_GROUPS:N_GROUPS + N_EXPERTS].astype(jnp.int32)
    padded = ((cnt + MOE_BLOCK - 1) // MOE_BLOCK) * MOE_BLOCK
    pad_end = jnp.cumsum(padded)
    pad_start = pad_end - padded
    ri = route[:, 0:4].astype(jnp.int32)
    start_of = jnp.sum(jnp.where(ri[:, 0:2, None] == jnp.arange(N_EXPERTS, dtype=jnp.int32), pad_start, 0), axis=-1)
    dest = (start_of + ri[:, 2:4]) * ROW_SLABS
    block_row = jnp.arange(nb, dtype=jnp.int32) * MOE_BLOCK
    block_exp = jnp.minimum(jnp.sum(pad_end[None, :] <= block_row[:, None], axis=1), N_EXPERTS - 1).astype(jnp.int32)
    nused = (pad_end[-1:] // MOE_BLOCK).astype(jnp.int32)
    xs = _dispatch(h, dest, nb * MOE_BLOCK)
    ys = _experts(xs, block_exp, nused, w_gate, w_up, w_down, layer)
    return _combine(ys, dest, x_mid, route)


def _mlstm_in_kernel(x_ref, g_ref, w_ref, wg_hi_ref, wg_lo_ref, gb_ref, qk_ref, v_ref, og_ref, gt_ref):
    hf = _rms(x_ref[...], g_ref[...])
    h = hf.astype(BF16)
    qk_ref[...] = _dot(h, w_ref[:, 0:D_MODEL])
    v_ref[...] = _dot(h, w_ref[:, D_MODEL:2 * D_MODEL]).astype(BF16)
    og_ref[...] = _dot(h, w_ref[:, 2 * D_MODEL:3 * D_MODEL]).astype(BF16)
    gt_ref[...] = _dot3(hf, wg_hi_ref[...], wg_lo_ref[...]) + gb_ref[...]


def _mlstm_in_proj(x, gain, w, wg_hi, wg_lo, gb):
    n = x.shape[0]
    tm = ROW_TILE
    row = lambda w_: pl.BlockSpec((tm, w_), lambda i: (i, 0))
    return pl.pallas_call(
        _mlstm_in_kernel,
        grid=(n // tm,),
        in_specs=[
            row(D_MODEL),
            _resident((1, D_MODEL)),
            _resident((D_MODEL, 3 * D_MODEL)),
            _resident((D_MODEL, LANES)),
            _resident((D_MODEL, LANES)),
            _resident((1, LANES)),
        ],
        out_specs=[row(D_MODEL), row(D_MODEL), row(D_MODEL), row(LANES)],
        out_shape=[
            jax.ShapeDtypeStruct((n, D_MODEL), F32),
            jax.ShapeDtypeStruct((n, D_MODEL), BF16),
            jax.ShapeDtypeStruct((n, D_MODEL), BF16),
            jax.ShapeDtypeStruct((n, LANES), F32),
        ],
        compiler_params=_params("parallel"),
        name="mlstm_in_proj",
    )(x, gain, w, wg_hi, wg_lo, gb)


def _log_sigmoid(x):
    return jnp.minimum(x, 0.0) - jnp.log(1.0 + jnp.exp(-jnp.abs(x)))


def _mlstm_kernel(qk_ref, v_ref, og_ref, gt_ref, cw_ref, cb_ref, hg_ref, tri_ref, y_ref,
                  cbuf, c_sc, n_sc, m_sc):
    L = MLSTM_CHUNK
    dk, dv = MLSTM_QK_DIM, MLSTM_V_DIM

    @pl.when(pl.program_id(1) == 0)
    def _():
        cbuf[0:8, :] = jnp.zeros((8, D_MODEL), F32)
        c_sc[...] = jnp.zeros_like(c_sc)
        n_sc[...] = jnp.zeros_like(n_sc)
        m_sc[...] = jnp.zeros_like(m_sc)

    cbuf[8:8 + L, :] = qk_ref[...]
    conv = cb_ref[...] + cw_ref[0:1, :] * cbuf[5:5 + L, :]
    for j in range(1, CONV_WIDTH):
        conv = conv + cw_ref[j:j + 1, :] * cbuf[5 + j:5 + j + L, :]
    cbuf[0:8, :] = cbuf[L:L + 8, :]
    qk = conv * (1.0 / (1.0 + jnp.exp(-conv)))

    gates = gt_ref[...]
    lf = _log_sigmoid(gates)
    lf_hi = lf.astype(BF16)
    lf_mid, lf_lo = _hi_lo(lf - lf_hi.astype(F32))
    tri = tri_ref[...]
    b = _dot(tri, lf_hi) + (_dot(tri, lf_mid) + _dot(tri, lf_lo))
    gates_t = gates.T
    b_t = b.T
    ti = lax.broadcasted_iota(jnp.int32, (L, L), 0)
    si = lax.broadcasted_iota(jnp.int32, (L, L), 1)
    causal = ti >= si

    for h in range(MLSTM_HEADS):
        i_col, i_row = gates[:, h:h + 1], gates_t[h:h + 1, :]
        b_col, b_row = b[:, 4 + h:5 + h], b_t[4 + h:5 + h, :]
        m_prev = m_sc[h, 0:1, 0:1]
        q = qk[:, h * dk:(h + 1) * dk]
        k = qk[:, D_MODEL // 2 + h * dk:D_MODEL // 2 + (h + 1) * dk] * (dk ** -0.5)
        qb, kb = q.astype(BF16), k.astype(BF16)
        v = v_ref[:, h * dv:(h + 1) * dv]

        dmat = jnp.where(causal, b_col - b_row + i_row, NEG)
        m_inter = b_col + m_prev
        m_t = jnp.maximum(m_inter, jnp.max(dmat, axis=-1, keepdims=True))
        s = _dot_nt(qb, kb) * jnp.exp(dmat - m_t)
        w_inter = jnp.exp(m_inter - m_t)
        num = _dot(s.astype(BF16), v) + w_inter * _dot(qb, c_sc[h].astype(BF16))
        den = jnp.sum(s, axis=-1, keepdims=True) + w_inter * jnp.sum(q * n_sc[h], axis=-1, keepdims=True)
        hh = num / jnp.maximum(jnp.abs(den), jnp.exp(-m_t))

        b_last = b_col[L - 1:L, :]
        g = b_last - b_col + i_col
        m_new = jnp.maximum(b_last + m_prev, jnp.max(g, axis=0, keepdims=True))
        decay = jnp.exp(b_last + m_prev - m_new)
        wk = jnp.exp(g - m_new) * k
        c_sc[h] = decay * c_sc[h] + _dot(wk.T.astype(BF16), v)
        n_sc[h] = decay * n_sc[h] + jnp.sum(wk, axis=0, keepdims=True)
        m_sc[h] = jnp.broadcast_to(m_new, (8, LANES))

        cols = slice(h * dv, (h + 1) * dv)
        hn = hh * lax.rsqrt(jnp.mean(hh * hh, axis=-1, keepdims=True) + NORM_EPS) * hg_ref[:, cols]
        og = og_ref[:, cols].astype(F32)
        y_ref[:, cols] = (hn * (1.0 / (1.0 + jnp.exp(-og)))).astype(BF16)


def _mlstm_cell(qk, v, og, gates, conv_w, conv_b, h_gain, batch, seq):
    L = MLSTM_CHUNK
    nc = seq // L
    row = lambda w_: pl.BlockSpec((L, w_), lambda b, c: (b * nc + c, 0))
    tri = (lax.broadcasted_iota(jnp.int32, (L, L), 0) >= lax.broadcasted_iota(jnp.int32, (L, L), 1)).astype(BF16)
    return pl.pallas_call(
        _mlstm_kernel,
        grid=(batch, nc),
        in_specs=[
            row(D_MODEL), row(D_MODEL), row(D_MODEL), row(LANES),
            _resident((CONV_WIDTH, D_MODEL)),
            _resident((1, D_MODEL)),
            _resident((1, D_MODEL)),
            _resident((L, L)),
        ],
        out_specs=row(D_MODEL),
        out_shape=jax.ShapeDtypeStruct((batch * seq, D_MODEL), BF16),
        scratch_shapes=[
            pltpu.VMEM((L + 8, D_MODEL), F32),
            pltpu.VMEM((MLSTM_HEADS, MLSTM_QK_DIM, MLSTM_V_DIM), F32),
            pltpu.VMEM((MLSTM_HEADS, 1, MLSTM_QK_DIM), F32),
            pltpu.VMEM((MLSTM_HEADS, 8, LANES), F32),
        ],
        compiler_params=_params("arbitrary", "arbitrary"),
        name="mlstm_cell",
    )(qk, v, og, gates, conv_w, conv_b, h_gain, tri)


def _pad_lanes(w):
    return jnp.pad(w, ((0, 0), (0, LANES - w.shape[1])))


def kernel(x, norm_mix, norm_ffn, attn_w_in, attn_q_norm, attn_k_norm, attn_w_out, mlstm_w_in, mlstm_conv_w, mlstm_conv_b, mlstm_gate_b, mlstm_h_norm, mlstm_w_out, moe_w_group, moe_b_group, moe_w_expert, moe_b_expert, moe_w_gate, moe_w_up, moe_w_down):
    batch, seq, d = x.shape
    assert d == D_MODEL and seq % (ATTN_BLOCK * max(DILATIONS)) == 0 and seq % MLSTM_CHUNK == 0
    assert (batch * seq) % GATHER_TILE == 0 and GATHER_TILE % ROW_TILE == 0
    depth = norm_mix.shape[0]
    n = batch * seq
    xf = x.reshape(n, d)

    head_of = jnp.arange(D_MODEL) // ATTN_HEAD_DIM
    bd = (head_of[:, None] == head_of[None, :]).astype(BF16)
    expand = (jnp.arange(LANES)[:, None] == head_of[None, :]).astype(BF16)

    for layer in range(depth):
        j = layer // N_MIXERS
        gain = norm_mix[layer].reshape(1, d)
        wr = jnp.concatenate([moe_w_group[layer], moe_w_expert[layer]], axis=1)
        wr_hi, wr_lo = _hi_lo(_pad_lanes(wr))
        rb = _pad_lanes(jnp.concatenate([moe_b_group[layer], moe_b_expert[layer]]).reshape(1, -1))
        ffn_gain = norm_ffn[layer].reshape(1, d)
        if layer % N_MIXERS == 0:
            qkg = jnp.concatenate([jnp.tile(attn_q_norm[j], ATTN_HEADS) * ATTN_HEAD_DIM ** -0.5,
                                   jnp.tile(attn_k_norm[j], ATTN_HEADS)]).reshape(1, 2 * d)
            qkvs = _attn_in_proj(xf, gain, attn_w_in[j].astype(BF16), bd, qkg)
            outs = [_attn_branch(qkv_d, batch, seq, dil) for qkv_d, dil in zip(qkvs, DILATIONS)]
            mix = tuple(o for o, _ in outs) + tuple(l for _, l in outs)
            x_mid, h, route, counts = _post_mixer(mix, xf, attn_w_out[j].astype(BF16), ffn_gain,
                                                  wr_hi, wr_lo, rb, expand=expand)
        else:
            w_in = mlstm_w_in[j]
            wg_hi, wg_lo = _hi_lo(_pad_lanes(w_in[:, 3 * d:]))
            gb = _pad_lanes(mlstm_gate_b[j].reshape(1, -1))
            qk, v, og, gates = _mlstm_in_proj(xf, gain, w_in[:, :3 * d].astype(BF16), wg_hi, wg_lo, gb)
            y = _mlstm_cell(qk, v, og, gates, mlstm_conv_w[j], mlstm_conv_b[j].reshape(1, d),
                            mlstm_h_norm[j].reshape(1, d), batch, seq)
            x_mid, h, route, counts = _post_mixer((y,), xf, mlstm_w_out[j].astype(BF16), ffn_gain,
                                                  wr_hi, wr_lo, rb)
        xf = _moe(x_mid, h, route, counts, moe_w_gate, moe_w_up, moe_w_down, layer)
    return xf.reshape(batch, seq, d)
```

```python
import functools

import jax
import jax.numpy as jnp
from jax import lax
from jax.experimental import pallas as pl
from jax.experimental.pallas import tpu as pltpu

F32 = jnp.float32
BF16 = jnp.bfloat16

D_MODEL = 1024
N_MIXERS = 2
ATTN_HEAD_DIM = 64
ATTN_HEADS = D_MODEL // ATTN_HEAD_DIM
DILATIONS = (1, 4, 16)
ATTN_BLOCK = 128
ATTN_QBLOCKS = 2
MLSTM_HEADS = 4
MLSTM_V_DIM = D_MODEL // MLSTM_HEADS
MLSTM_QK_DIM = MLSTM_V_DIM // 2
MLSTM_CHUNK = 256
CONV_WIDTH = 4
N_GROUPS = 8
EXPERTS_PER_GROUP = 8
N_EXPERTS = N_GROUPS * EXPERTS_PER_GROUP
EXPERT_FF = D_MODEL // 2
MOE_BLOCK = 256
NORM_EPS = 1e-6

LANES = 128
ROW_TILE = 256
GATHER_TILE = 512
VMEM_LIMIT = 48 * 1024 * 1024
NEG = -1e30


def _dot(a, b):
    return jnp.dot(a, b, preferred_element_type=F32)


def _dot_nt(a, b):
    return lax.dot_general(a, b, (((1,), (1,)), ((), ())), preferred_element_type=F32)


def _rms(x, gain):
    ms = jnp.mean(x * x, axis=-1, keepdims=True)
    return x * lax.rsqrt(ms + NORM_EPS) * gain


def _hi_lo(x):
    hi = x.astype(BF16)
    lo = (x - hi.astype(F32)).astype(BF16)
    return hi, lo


def _dot3(x, w_hi, w_lo):
    x_hi, x_lo = _hi_lo(x)
    return _dot(x_hi, w_hi) + (_dot(x_lo, w_hi) + _dot(x_hi, w_lo))


ROW_SLABS = D_MODEL // LANES


def _store_rows(ref, x):
    rows = x.shape[0]
    for j in range(ROW_SLABS):
        ref[pl.ds(j, rows, stride=ROW_SLABS), :] = x[:, j * LANES:(j + 1) * LANES]


def _load_rows(ref, rows):
    return jnp.concatenate([ref[pl.ds(j, rows, stride=ROW_SLABS), :] for j in range(ROW_SLABS)], axis=1)


def _resident(shape):
    nd = len(shape)
    return pl.BlockSpec(shape, lambda *_: (0,) * nd, pipeline_mode=pl.Buffered(1))


def _params(*sem):
    return pltpu.CompilerParams(dimension_semantics=sem, vmem_limit_bytes=VMEM_LIMIT)


def _dilation_perm(tm, dil):
    i = jnp.arange(tm)
    src = (i % (tm // dil)) * dil + i // (tm // dil)
    return (src[:, None] == jnp.arange(tm)[None, :]).astype(BF16)


def _attn_in_kernel(x_ref, g_ref, w_ref, bd_ref, qkg_ref, *rest):
    nd = len(DILATIONS)
    perm_refs, o_refs = rest[:nd - 1], rest[nd - 1:]
    tm = x_ref.shape[0]
    h = _rms(x_ref[...], g_ref[...]).astype(BF16)
    for c in range(3):
        cols = slice(c * D_MODEL, (c + 1) * D_MODEL)
        p = _dot(h, w_ref[:, cols])
        if c < 2:
            ss = _dot((p * p).astype(BF16), bd_ref[...])
            p = p * lax.rsqrt(ss * (1.0 / ATTN_HEAD_DIM) + NORM_EPS) * qkg_ref[:, cols]
        pb = p.astype(BF16)
        o_refs[0][:, cols] = pb
        for o_ref, perm_ref, dil in zip(o_refs[1:], perm_refs, DILATIONS[1:]):
            pp = _dot(perm_ref[...], pb).astype(BF16)
            for r in range(dil):
                dst = (3 * r + c) * D_MODEL
                o_ref[:, dst:dst + D_MODEL] = pp[r * (tm // dil):(r + 1) * (tm // dil), :]


def _attn_in_proj(x, gain, w, bd, qkg):
    n = x.shape[0]
    tm = ROW_TILE
    assert DILATIONS[0] == 1
    perms = [_dilation_perm(tm, dil) for dil in DILATIONS[1:]]
    return pl.pallas_call(
        _attn_in_kernel,
        grid=(n // tm,),
        in_specs=[
            pl.BlockSpec((tm, D_MODEL), lambda i: (i, 0)),
            _resident((1, D_MODEL)),
            _resident((D_MODEL, 3 * D_MODEL)),
            _resident((D_MODEL, D_MODEL)),
            _resident((1, 2 * D_MODEL)),
        ] + [_resident((tm, tm))] * len(perms),
        out_specs=[pl.BlockSpec((tm // dil, dil * 3 * D_MODEL), lambda i: (i, 0)) for dil in DILATIONS],
        out_shape=[jax.ShapeDtypeStruct((n // dil, dil * 3 * D_MODEL), BF16) for dil in DILATIONS],
        compiler_params=_params("parallel"),
        name="attn_in_proj",
    )(x, gain, w, bd, qkg, *perms)


def _attn_kernel(q_ref, kp_ref, kc_ref, vp_ref, vc_ref, o_ref, lse_ref, s_sc, m_sc):
    blk = ATTN_BLOCK
    for qb in range(ATTN_QBLOCKS):
        cur = slice(qb * blk, (qb + 1) * blk)
        if qb == 0:
            prev_k, prev_v, has_prev = kp_ref, vp_ref, pl.program_id(2) > 0
        else:
            prev = slice((qb - 1) * blk, qb * blk)
            prev_k, prev_v, has_prev = kc_ref.at[prev], vc_ref.at[prev], True
        _attn_block(q_ref.at[cur], prev_k, kc_ref.at[cur], prev_v, vc_ref.at[cur], has_prev,
                    o_ref.at[cur], lse_ref.at[cur], s_sc, m_sc)


def _attn_block(q_ref, kp_ref, kc_ref, vp_ref, vc_ref, has_prev, o_ref, lse_ref, s_sc, m_sc):
    blk = ATTN_BLOCK
    qi = lax.broadcasted_iota(jnp.int32, (blk, blk), 0)
    ki = lax.broadcasted_iota(jnp.int32, (blk, blk), 1)
    bias = jnp.concatenate([jnp.where((ki >= qi) & has_prev, 0.0, NEG), jnp.where(ki <= qi, 0.0, NEG)], axis=1)
    lane = lax.broadcasted_iota(jnp.int32, (1, LANES), 1)
    first = lane < ATTN_HEAD_DIM
    ones = jnp.ones((2 * blk, LANES), BF16)

    for pr in range(ATTN_HEADS // 2):
        cols = slice(pr * LANES, (pr + 1) * LANES)
        q2 = q_ref[:, cols]
        k_band = jnp.concatenate([kp_ref[:, cols], kc_ref[:, cols]], axis=0)
        for hh in range(2):
            h = 2 * pr + hh
            mine = first if hh == 0 else jnp.logical_not(first)
            qm = jnp.where(mine, q2, jnp.zeros_like(q2))
            s = _dot_nt(qm, k_band) + bias
            s_sc[h] = s
            m_sc[h] = jnp.broadcast_to(jnp.max(jnp.maximum(s[:, :blk], s[:, blk:]), axis=-1, keepdims=True),
                                       (blk, LANES))

    m_acc = jnp.zeros((blk, LANES), F32)
    l_acc = jnp.ones((blk, LANES), F32)
    for pr in range(ATTN_HEADS // 2):
        cols = slice(pr * LANES, (pr + 1) * LANES)
        v_band = jnp.concatenate([vp_ref[:, cols], vc_ref[:, cols]], axis=0)
        v_ones = jnp.concatenate([v_band, ones], axis=1)
        o_pair = None
        for hh in range(2):
            h = 2 * pr + hh
            m = m_sc[h]
            p = jnp.concatenate([jnp.exp(s_sc[h, :, 0:blk] - m).astype(BF16),
                                 jnp.exp(s_sc[h, :, blk:2 * blk] - m).astype(BF16)], axis=1)
            ol = _dot(p, v_ones)
            l = ol[:, LANES:]
            o2 = ol[:, :LANES] * (1.0 / l)
            o_pair = o2 if hh == 0 else jnp.where(first, o_pair, o2)
            m_acc = jnp.where(lane == h, m, m_acc)
            l_acc = jnp.where(lane == h, l, l_acc)
        o_ref[:, cols] = o_pair.astype(BF16)
    lse_ref[...] = jnp.where(lane < ATTN_HEADS, m_acc + jnp.log(l_acc), 0.0)


def _attn_branch(qkv_d, batch, seq, dil):
    blk = ATTN_BLOCK
    qb = ATTN_QBLOCKS
    rows = seq // dil
    steps = rows // (qb * blk)
    assert rows % (qb * blk) == 0
    qkv_v = qkv_d.reshape(batch, rows, dil * 3 * D_MODEL)

    def spec(c, prev):
        if prev:
            return pl.BlockSpec((None, blk, D_MODEL), lambda b, r, n: (b, jnp.maximum(qb * n - 1, 0), 3 * r + c))
        return pl.BlockSpec((None, qb * blk, D_MODEL), lambda b, r, n: (b, n, 3 * r + c))

    o, lse = pl.pallas_call(
        _attn_kernel,
        grid=(batch, dil, steps),
        in_specs=[spec(0, False), spec(1, True), spec(1, False), spec(2, True), spec(2, False)],
        out_specs=[
            pl.BlockSpec((None, qb * blk, D_MODEL), lambda b, r, n: (b, n, r)),
            pl.BlockSpec((None, qb * blk, LANES), lambda b, r, n: (b, n, r)),
        ],
        out_shape=[
            jax.ShapeDtypeStruct((batch, rows, dil * D_MODEL), BF16),
            jax.ShapeDtypeStruct((batch, rows, dil * LANES), F32),
        ],
        scratch_shapes=[
            pltpu.VMEM((ATTN_HEADS, blk, 2 * blk), F32),
            pltpu.VMEM((ATTN_HEADS, blk, LANES), F32),
        ],
        compiler_params=_params("parallel", "parallel", "arbitrary"),
        name=f"attn_dil{dil}",
    )(qkv_v, qkv_v, qkv_v, qkv_v, qkv_v)
    return o.reshape(batch * rows, dil * D_MODEL), lse.reshape(batch * rows, dil * LANES)


def _post_kernel(is_attn, *refs):
    if is_attn:
        o_refs, l_refs, e_ref, unperm_refs = refs[0:3], refs[3:6], refs[6], refs[7:9]
        refs = refs[9:]
    else:
        y_ref = refs[0]
        refs = refs[1:]
    (wout_ref, x_ref, gain_ref, wr_hi_ref, wr_lo_ref, rb_ref, tri_ref,
     xo_ref, h_ref, route_ref, cnt_ref, base_ref) = refs[:12]

    if is_attn:
        l_sc = refs[12]
        tm = x_ref.shape[0]
        for g, dil in enumerate(DILATIONS):
            for r in range(dil):
                l_sc[g, pl.ds(r, tm // dil, stride=dil), :] = l_refs[g][:, r * LANES:(r + 1) * LANES]
        ls = [l_sc[g] for g in range(3)]
        mx = jnp.maximum(jnp.maximum(ls[0], ls[1]), ls[2])
        es = [jnp.exp(l - mx) for l in ls]
        inv = 1.0 / (es[0] + es[1] + es[2])
        y = None
        for g, dil in enumerate(DILATIONS):
            w_hi, w_lo = _hi_lo(es[g] * inv)
            w_full = _dot(w_hi, e_ref[...]) + _dot(w_lo, e_ref[...])
            if dil == 1:
                o_g = o_refs[g][...].astype(F32)
            else:
                o_rm = jnp.concatenate([o_refs[g][:, r * D_MODEL:(r + 1) * D_MODEL] for r in range(dil)], axis=0)
                o_g = _dot(unperm_refs[g - 1][...], o_rm)
            t = w_full * o_g
            y = t if y is None else y + t
        y = y.astype(BF16)
    else:
        y = y_ref[...]

    xo = x_ref[...] + _dot(y, wout_ref[...])
    xo_ref[...] = xo
    hf = _rms(xo, gain_ref[...])
    _store_rows(h_ref, hf)

    logits = _dot3(hf, wr_hi_ref[...], wr_lo_ref[...]) + rb_ref[...]
    lane = lax.broadcasted_iota(jnp.int32, (1, LANES), 1)
    lanef = lane.astype(F32)
    is_grp = lane < N_GROUPS
    gl = jnp.where(is_grp, logits, NEG)
    gmax = jnp.max(gl, axis=-1, keepdims=True)
    gidx = jnp.min(jnp.where(gl == gmax, lanef, float(LANES)), axis=-1, keepdims=True)
    p_grp = 1.0 / jnp.sum(jnp.where(is_grp, jnp.exp(gl - gmax), 0.0), axis=-1, keepdims=True)
    lane_grp = ((lane - N_GROUPS) >> 3).astype(F32)
    in_grp = (lane >= N_GROUPS) & (lane < N_GROUPS + N_EXPERTS) & (lane_grp == gidx)
    em = jnp.where(in_grp, logits, NEG)
    v1 = jnp.max(em, axis=-1, keepdims=True)
    i1 = jnp.min(jnp.where(em == v1, lanef, float(LANES)), axis=-1, keepdims=True)
    hot0 = lanef == i1
    em2 = jnp.where(hot0, NEG, em)
    v2 = jnp.max(em2, axis=-1, keepdims=True)
    i2 = jnp.min(jnp.where(em2 == v2, lanef, float(LANES)), axis=-1, keepdims=True)
    hot1 = lanef == i2
    t2 = jnp.exp(v2 - v1)
    g0 = p_grp / (1.0 + t2)
    g1 = g0 * t2

    @pl.when(pl.program_id(0) == 0)
    def _():
        base_ref[...] = jnp.zeros_like(base_ref)

    hot = jnp.where(hot0 | hot1, 1.0, 0.0)
    before = _dot(tri_ref[...], hot.astype(BF16)) + base_ref[0:1, :]
    rank0 = jnp.sum(jnp.where(hot0, before, 0.0), axis=-1, keepdims=True)
    rank1 = jnp.sum(jnp.where(hot1, before, 0.0), axis=-1, keepdims=True)
    total = base_ref[0:1, :] + jnp.sum(hot, axis=0, keepdims=True)
    base_ref[...] = jnp.broadcast_to(total, base_ref.shape)
    cnt_ref[...] = jnp.broadcast_to(total, cnt_ref.shape)

    route = jnp.where(lane == 0, i1 - N_GROUPS, 0.0)
    route = jnp.where(lane == 1, i2 - N_GROUPS, route)
    route = jnp.where(lane == 2, rank0, route)
    route = jnp.where(lane == 3, rank1, route)
    route = jnp.where(lane == 4, g0, route)
    route = jnp.where(lane == 5, g1, route)
    route_ref[...] = route


def _post_mixer(mix, x, wout, gain, wr_hi, wr_lo, rb, expand=None):
    n = x.shape[0]
    tm = ROW_TILE
    is_attn = expand is not None
    row = lambda w: pl.BlockSpec((tm, w), lambda i: (i, 0))
    tri = (lax.broadcasted_iota(jnp.int32, (tm, tm), 0) > lax.broadcasted_iota(jnp.int32, (tm, tm), 1)).astype(BF16)
    if is_attn:
        dil_rows = lambda w: [pl.BlockSpec((tm // dil, dil * w), lambda i: (i, 0)) for dil in DILATIONS]
        unperms = [_dilation_perm(tm, dil).T for dil in DILATIONS[1:]]
        mix_specs = (dil_rows(D_MODEL) + dil_rows(LANES) + [_resident((LANES, D_MODEL))]
                     + [_resident((tm, tm))] * len(unperms))
        mix_args = tuple(mix) + (expand,) + tuple(unperms)
        scratch = [pltpu.VMEM((3, tm, LANES), F32)]
    else:
        mix_specs = [row(D_MODEL)]
        mix_args = tuple(mix)
        scratch = []
    return pl.pallas_call(
        functools.partial(_post_kernel, is_attn),
        grid=(n // tm,),
        in_specs=mix_specs + [
            _resident((D_MODEL, D_MODEL)),
            row(D_MODEL),
            _resident((1, D_MODEL)),
            _resident((D_MODEL, LANES)),
            _resident((D_MODEL, LANES)),
            _resident((1, LANES)),
            _resident((tm, tm)),
        ],
        out_specs=[row(D_MODEL), pl.BlockSpec((tm * ROW_SLABS, LANES), lambda i: (i, 0)), row(LANES),
                   pl.BlockSpec((8, LANES), lambda i: (0, 0))],
        out_shape=[
            jax.ShapeDtypeStruct((n, D_MODEL), F32),
            jax.ShapeDtypeStruct((n * ROW_SLABS, LANES), F32),
            jax.ShapeDtypeStruct((n, LANES), F32),
            jax.ShapeDtypeStruct((8, LANES), F32),
        ],
        scratch_shapes=[pltpu.VMEM((8, LANES), F32)] + scratch,
        compiler_params=_params("arbitrary"),
        name="post_attn" if is_attn else "post_mlstm",
    )(*mix_args, wout, x, gain, wr_hi, wr_lo, rb, tri)


PAD_CHUNKS = tuple(1 << b for b in reversed(range(MOE_BLOCK.bit_length() - 1)))


def _dispatch_kernel(dest_ref, pad_row_ref, pad_len_ref, nused_ref, h_ref, xs_hbm, zeros_sc, sem, zero_sem):
    tm = GATHER_TILE

    @pl.when(pl.program_id(0) == 0)
    def _():
        zeros_sc[...] = jnp.zeros_like(zeros_sc)
        block = MOE_BLOCK * ROW_SLABS

        def tail_copy(b):
            return pltpu.make_async_copy(zeros_sc, xs_hbm.at[pl.ds(pl.multiple_of(b * block, block), block)], zero_sem)

        def start_tail(b, c):
            tail_copy(b).start()
            return c

        def wait_tail(b, c):
            tail_copy(b).wait()
            return c

        lax.fori_loop(nused_ref[0], xs_hbm.shape[0] // block, start_tail, 0)
        lax.fori_loop(nused_ref[0], xs_hbm.shape[0] // block, wait_tail, 0)

        def for_each_run(e, action):
            row, length = pad_row_ref[e], pad_len_ref[e]
            for chunk in PAD_CHUNKS:
                used = length & chunk

                @pl.when(used != 0)
                def _(row=row, chunk=chunk):
                    dst = pl.multiple_of(row * ROW_SLABS, ROW_SLABS)
                    action(pltpu.make_async_copy(zeros_sc.at[pl.ds(0, chunk * ROW_SLABS)],
                                                 xs_hbm.at[pl.ds(dst, chunk * ROW_SLABS)], zero_sem))
                row = row + used

        def start_runs(e, c):
            for_each_run(e, lambda cp: cp.start())
            return c

        def wait_runs(e, c):
            for_each_run(e, lambda cp: cp.wait())
            return c

        lax.fori_loop(0, N_EXPERTS, start_runs, 0)
        lax.fori_loop(0, N_EXPERTS, wait_runs, 0)

    def copy(t, k):
        src = pl.multiple_of(t * ROW_SLABS, ROW_SLABS)
        dst = pl.multiple_of(dest_ref[0, 0, 2 * t + k], ROW_SLABS)
        return pltpu.make_async_copy(h_ref.at[pl.ds(src, ROW_SLABS)], xs_hbm.at[pl.ds(dst, ROW_SLABS)], sem)

    def start(t, c):
        copy(t, 0).start()
        copy(t, 1).start()
        return c

    def wait(t, c):
        copy(t, 0).wait()
        copy(t, 1).wait()
        return c

    lax.fori_loop(0, tm, start, 0, unroll=4)
    lax.fori_loop(0, tm, wait, 0, unroll=4)


def _dispatch(h, dest, pad_row, pad_len, nused, p_rows):
    n = h.shape[0] // ROW_SLABS
    tm = GATHER_TILE
    return pl.pallas_call(
        _dispatch_kernel,
        grid=(n // tm,),
        in_specs=[
            pl.BlockSpec((1, 1, 2 * tm), lambda i: (i, 0, 0), memory_space=pltpu.SMEM),
            pl.BlockSpec(memory_space=pltpu.SMEM),
            pl.BlockSpec(memory_space=pltpu.SMEM),
            pl.BlockSpec(memory_space=pltpu.SMEM),
            pl.BlockSpec((tm * ROW_SLABS, LANES), lambda i: (i, 0)),
        ],
        out_specs=pl.BlockSpec(memory_space=pl.ANY),
        out_shape=jax.ShapeDtypeStruct((p_rows * ROW_SLABS, LANES), F32),
        scratch_shapes=[pltpu.VMEM((MOE_BLOCK * ROW_SLABS, LANES), F32),
                        pltpu.SemaphoreType.DMA(()), pltpu.SemaphoreType.DMA(())],
        compiler_params=_params("arbitrary"),
        name="moe_dispatch",
    )(dest.reshape(n // tm, 1, 2 * tm), pad_row, pad_len, nused, h)


def _expert_kernel(bexp_ref, nused_ref, x_ref, wg_ref, wu_ref, wd_ref, o_ref):
    del bexp_ref
    is_live = pl.program_id(0) < nused_ref[0]

    @pl.when(jnp.logical_not(is_live))
    def _():
        o_ref[...] = jnp.zeros_like(o_ref)

    @pl.when(is_live)
    def _():
        xb = _load_rows(x_ref, MOE_BLOCK).astype(BF16)
        g = _dot(xb, wg_ref[...].astype(BF16))
        u = _dot(xb, wu_ref[...].astype(BF16))
        a = (g * (1.0 / (1.0 + jnp.exp(-g))) * u).astype(BF16)
        _store_rows(o_ref, _dot(a, wd_ref[...].astype(BF16)))


def _experts(xs, block_exp, nused, w_gate, w_up, w_down, layer):
    nb = xs.shape[0] // (MOE_BLOCK * ROW_SLABS)
    live = lambda i, be, nu: jnp.minimum(i, nu[0] - 1)
    wsel = lambda i, be, nu: (layer, be[live(i, be, nu)], 0, 0)
    return pl.pallas_call(
        _expert_kernel,
        grid_spec=pltpu.PrefetchScalarGridSpec(
            num_scalar_prefetch=2,
            grid=(nb,),
            in_specs=[
                pl.BlockSpec((MOE_BLOCK * ROW_SLABS, LANES), lambda i, be, nu: (live(i, be, nu), 0)),
                pl.BlockSpec((None, None, D_MODEL, EXPERT_FF), wsel),
                pl.BlockSpec((None, None, D_MODEL, EXPERT_FF), wsel),
                pl.BlockSpec((None, None, EXPERT_FF, D_MODEL), wsel),
            ],
            out_specs=pl.BlockSpec((MOE_BLOCK * ROW_SLABS, LANES), lambda i, be, nu: (i, 0)),
        ),
        out_shape=jax.ShapeDtypeStruct(xs.shape, F32),
        compiler_params=_params("arbitrary"),
        name="moe_experts",
    )(block_exp, nused, xs, w_gate, w_up, w_down)


def _combine_kernel(dcur_ref, dnext_ref, ys_hbm, x_ref, route_ref, o_ref, buf, sems):
    tm = GATHER_TILE
    i = pl.program_id(0)
    slot = i % 2

    def copy(d_ref, s, t, k):
        src = pl.multiple_of(d_ref[0, 0, 2 * t + k], ROW_SLABS)
        dst = pl.multiple_of(t * ROW_SLABS, ROW_SLABS)
        return pltpu.make_async_copy(ys_hbm.at[pl.ds(src, ROW_SLABS)],
                                     buf.at[s, k, pl.ds(dst, ROW_SLABS)], sems.at[s])

    def gather(d_ref, s):
        def start(t, c):
            copy(d_ref, s, t, 0).start()
            copy(d_ref, s, t, 1).start()
            return c
        lax.fori_loop(0, tm, start, 0, unroll=4)

    @pl.when(i == 0)
    def _():
        gather(dcur_ref, 0)

    @pl.when(i + 1 < pl.num_programs(0))
    def _():
        gather(dnext_ref, 1 - slot)

    def wait(t, c):
        copy(dcur_ref, slot, t, 0).wait()
        copy(dcur_ref, slot, t, 1).wait()
        return c

    lax.fori_loop(0, tm, wait, 0, unroll=4)
    r = route_ref[...]
    y0 = _load_rows(buf.at[slot, 0], tm)
    y1 = _load_rows(buf.at[slot, 1], tm)
    o_ref[...] = x_ref[...] + r[:, 4:5] * y0 + r[:, 5:6] * y1


def _combine(ys, dest, x, route):
    n = x.shape[0]
    tm = GATHER_TILE
    nt = n // tm
    dest_t = dest.reshape(nt, 1, 2 * tm)
    return pl.pallas_call(
        _combine_kernel,
        grid=(nt,),
        in_specs=[
            pl.BlockSpec((1, 1, 2 * tm), lambda i: (i, 0, 0), memory_space=pltpu.SMEM),
            pl.BlockSpec((1, 1, 2 * tm), lambda i: (jnp.minimum(i + 1, nt - 1), 0, 0), memory_space=pltpu.SMEM),
            pl.BlockSpec(memory_space=pl.ANY),
            pl.BlockSpec((tm, D_MODEL), lambda i: (i, 0)),
            pl.BlockSpec((tm, LANES), lambda i: (i, 0)),
        ],
        out_specs=pl.BlockSpec((tm, D_MODEL), lambda i: (i, 0)),
        out_shape=jax.ShapeDtypeStruct((n, D_MODEL), F32),
        scratch_shapes=[pltpu.VMEM((2, 2, tm * ROW_SLABS, LANES), F32), pltpu.SemaphoreType.DMA((2,))],
        compiler_params=_params("arbitrary"),
        name="moe_combine",
    )(dest_t, dest_t, ys, x, route)


def _moe(x_mid, h, route, counts, w_gate, w_up, w_down, layer):
    n = x_mid.shape[0]
    nb = (n * 2) // MOE_BLOCK + N_EXPERTS
    cnt = counts[0, N_GROUPS:N_GROUPS + N_EXPERTS].astype(jnp.int32)
    padded = ((cnt + MOE_BLOCK - 1) // MOE_BLOCK) * MOE_BLOCK
    pad_end = jnp.cumsum(padded)
    pad_start = pad_end - padded
    ri = route[:, 0:4].astype(jnp.int32)
    start_of = jnp.sum(jnp.where(ri[:, 0:2, None] == jnp.arange(N_EXPERTS, dtype=jnp.int32), pad_start, 0), axis=-1)
    dest = (start_of + ri[:, 2:4]) * ROW_SLABS
    block_row = jnp.arange(nb, dtype=jnp.int32) * MOE_BLOCK
    block_exp = jnp.minimum(jnp.sum(pad_end[None, :] <= block_row[:, None], axis=1), N_EXPERTS - 1).astype(jnp.int32)
    nused = (pad_end[-1:] // MOE_BLOCK).astype(jnp.int32)
    xs = _dispatch(h, dest, pad_start + cnt, padded - cnt, nused, nb * MOE_BLOCK)
    ys = _experts(xs, block_exp, nused, w_gate, w_up, w_down, layer)
    return _combine(ys, dest, x_mid, route)


def _mlstm_in_kernel(x_ref, g_ref, w_ref, wg_hi_ref, wg_lo_ref, gb_ref, qk_ref, v_ref, og_ref, gt_ref):
    hf = _rms(x_ref[...], g_ref[...])
    h = hf.astype(BF16)
    qk_ref[...] = _dot(h, w_ref[:, 0:D_MODEL])
    v_ref[...] = _dot(h, w_ref[:, D_MODEL:2 * D_MODEL]).astype(BF16)
    og_ref[...] = _dot(h, w_ref[:, 2 * D_MODEL:3 * D_MODEL]).astype(BF16)
    gt_ref[...] = _dot3(hf, wg_hi_ref[...], wg_lo_ref[...]) + gb_ref[...]


def _mlstm_in_proj(x, gain, w, wg_hi, wg_lo, gb):
    n = x.shape[0]
    tm = ROW_TILE
    row = lambda w_: pl.BlockSpec((tm, w_), lambda i: (i, 0))
    return pl.pallas_call(
        _mlstm_in_kernel,
        grid=(n // tm,),
        in_specs=[
            row(D_MODEL),
            _resident((1, D_MODEL)),
            _resident((D_MODEL, 3 * D_MODEL)),
            _resident((D_MODEL, LANES)),
            _resident((D_MODEL, LANES)),
            _resident((1, LANES)),
        ],
        out_specs=[row(D_MODEL), row(D_MODEL), row(D_MODEL), row(LANES)],
        out_shape=[
            jax.ShapeDtypeStruct((n, D_MODEL), F32),
            jax.ShapeDtypeStruct((n, D_MODEL), BF16),
            jax.ShapeDtypeStruct((n, D_MODEL), BF16),
            jax.ShapeDtypeStruct((n, LANES), F32),
        ],
        compiler_params=_params("parallel"),
        name="mlstm_in_proj",
    )(x, gain, w, wg_hi, wg_lo, gb)


def _log_sigmoid(x):
    return jnp.minimum(x, 0.0) - jnp.log(1.0 + jnp.exp(-jnp.abs(x)))


def _mlstm_kernel(qk_ref, v_ref, og_ref, gt_ref, cw_ref, cb_ref, hg_ref, tri_ref, y_ref,
                  cbuf, c_sc, n_sc, m_sc):
    L = MLSTM_CHUNK
    dk, dv = MLSTM_QK_DIM, MLSTM_V_DIM

    @pl.when(pl.program_id(1) == 0)
    def _():
        cbuf[0:8, :] = jnp.zeros((8, D_MODEL), F32)
        c_sc[...] = jnp.zeros_like(c_sc)
        n_sc[...] = jnp.zeros_like(n_sc)
        m_sc[...] = jnp.zeros_like(m_sc)

    cbuf[8:8 + L, :] = qk_ref[...]
    conv = cb_ref[...] + cw_ref[0:1, :] * cbuf[5:5 + L, :]
    for j in range(1, CONV_WIDTH):
        conv = conv + cw_ref[j:j + 1, :] * cbuf[5 + j:5 + j + L, :]
    cbuf[0:8, :] = cbuf[L:L + 8, :]
    qk = conv * (1.0 / (1.0 + jnp.exp(-conv)))

    gates = gt_ref[...]
    lf = _log_sigmoid(gates)
    lf_hi = lf.astype(BF16)
    lf_mid, lf_lo = _hi_lo(lf - lf_hi.astype(F32))
    tri = tri_ref[...]
    b = _dot(tri, lf_hi) + (_dot(tri, lf_mid) + _dot(tri, lf_lo))
    gates_t = gates.T
    b_t = b.T
    ti = lax.broadcasted_iota(jnp.int32, (L, L), 0)
    si = lax.broadcasted_iota(jnp.int32, (L, L), 1)
    causal = ti >= si

    for h in range(MLSTM_HEADS):
        i_col, i_row = gates[:, h:h + 1], gates_t[h:h + 1, :]
        b_col, b_row = b[:, 4 + h:5 + h], b_t[4 + h:5 + h, :]
        m_prev = m_sc[h, 0:1, 0:1]
        q = qk[:, h * dk:(h + 1) * dk]
        k = qk[:, D_MODEL // 2 + h * dk:D_MODEL // 2 + (h + 1) * dk] * (dk ** -0.5)
        qb, kb = q.astype(BF16), k.astype(BF16)
        v = v_ref[:, h * dv:(h + 1) * dv]

        dmat = jnp.where(causal, b_col - b_row + i_row, NEG)
        m_inter = b_col + m_prev
        m_t = jnp.maximum(m_inter, jnp.max(dmat, axis=-1, keepdims=True))
        s = _dot_nt(qb, kb) * jnp.exp(dmat - m_t)
        w_inter = jnp.exp(m_inter - m_t)
        num = _dot(s.astype(BF16), v) + w_inter * _dot(qb, c_sc[h].astype(BF16))
        den = jnp.sum(s, axis=-1, keepdims=True) + w_inter * jnp.sum(q * n_sc[h], axis=-1, keepdims=True)
        hh = num / jnp.maximum(jnp.abs(den), jnp.exp(-m_t))

        b_last = b_col[L - 1:L, :]
        g = b_last - b_col + i_col
        m_new = jnp.maximum(b_last + m_prev, jnp.max(g, axis=0, keepdims=True))
        decay = jnp.exp(b_last + m_prev - m_new)
        wk = jnp.exp(g - m_new) * k
        c_sc[h] = decay * c_sc[h] + _dot(wk.T.astype(BF16), v)
        n_sc[h] = decay * n_sc[h] + jnp.sum(wk, axis=0, keepdims=True)
        m_sc[h] = jnp.broadcast_to(m_new, (8, LANES))

        cols = slice(h * dv, (h + 1) * dv)
        hn = hh * lax.rsqrt(jnp.mean(hh * hh, axis=-1, keepdims=True) + NORM_EPS) * hg_ref[:, cols]
        og = og_ref[:, cols].astype(F32)
        y_ref[:, cols] = (hn * (1.0 / (1.0 + jnp.exp(-og)))).astype(BF16)


def _mlstm_cell(qk, v, og, gates, conv_w, conv_b, h_gain, batch, seq):
    L = MLSTM_CHUNK
    nc = seq // L
    row = lambda w_: pl.BlockSpec((L, w_), lambda b, c: (b * nc + c, 0))
    tri = (lax.broadcasted_iota(jnp.int32, (L, L), 0) >= lax.broadcasted_iota(jnp.int32, (L, L), 1)).astype(BF16)
    return pl.pallas_call(
        _mlstm_kernel,
        grid=(batch, nc),
        in_specs=[
            row(D_MODEL), row(D_MODEL), row(D_MODEL), row(LANES),
            _resident((CONV_WIDTH, D_MODEL)),
            _resident((1, D_MODEL)),
            _resident((1, D_MODEL)),
            _resident((L, L)),
        ],
        out_specs=row(D_MODEL),
        out_shape=jax.ShapeDtypeStruct((batch * seq, D_MODEL), BF16),
        scratch_shapes=[
            pltpu.VMEM((L + 8, D_MODEL), F32),
            pltpu.VMEM((MLSTM_HEADS, MLSTM_QK_DIM, MLSTM_V_DIM), F32),
            pltpu.VMEM((MLSTM_HEADS, 1, MLSTM_QK_DIM), F32),
            pltpu.VMEM((MLSTM_HEADS, 8, LANES), F32),
        ],
        compiler_params=_params("arbitrary", "arbitrary"),
        name="mlstm_cell",
    )(qk, v, og, gates, conv_w, conv_b, h_gain, tri)


def _pad_lanes(w):
    return jnp.pad(w, ((0, 0), (0, LANES - w.shape[1])))


def kernel(x, norm_mix, norm_ffn, attn_w_in, attn_q_norm, attn_k_norm, attn_w_out, mlstm_w_in, mlstm_conv_w, mlstm_conv_b, mlstm_gate_b, mlstm_h_norm, mlstm_w_out, moe_w_group, moe_b_group, moe_w_expert, moe_b_expert, moe_w_gate, moe_w_up, moe_w_down):
    batch, seq, d = x.shape
    assert d == D_MODEL and seq % (ATTN_BLOCK * max(DILATIONS)) == 0 and seq % MLSTM_CHUNK == 0
    assert (batch * seq) % GATHER_TILE == 0 and GATHER_TILE % ROW_TILE == 0
    depth = norm_mix.shape[0]
    n = batch * seq
    xf = x.reshape(n, d)

    head_of = jnp.arange(D_MODEL) // ATTN_HEAD_DIM
    bd = (head_of[:, None] == head_of[None, :]).astype(BF16)
    expand = (jnp.arange(LANES)[:, None] == head_of[None, :]).astype(BF16)

    for layer in range(depth):
        j = layer // N_MIXERS
        gain = norm_mix[layer].reshape(1, d)
        wr = jnp.concatenate([moe_w_group[layer], moe_w_expert[layer]], axis=1)
        wr_hi, wr_lo = _hi_lo(_pad_lanes(wr))
        rb = _pad_lanes(jnp.concatenate([moe_b_group[layer], moe_b_expert[layer]]).reshape(1, -1))
        ffn_gain = norm_ffn[layer].reshape(1, d)
        if layer % N_MIXERS == 0:
            qkg = jnp.concatenate([jnp.tile(attn_q_norm[j], ATTN_HEADS) * ATTN_HEAD_DIM ** -0.5,
                                   jnp.tile(attn_k_norm[j], ATTN_HEADS)]).reshape(1, 2 * d)
            qkvs = _attn_in_proj(xf, gain, attn_w_in[j].astype(BF16), bd, qkg)
            outs = [_attn_branch(qkv_d, batch, seq, dil) for qkv_d, dil in zip(qkvs, DILATIONS)]
            mix = tuple(o for o, _ in outs) + tuple(l for _, l in outs)
            x_mid, h, route, counts = _post_mixer(mix, xf, attn_w_out[j].astype(BF16), ffn_gain,
                                                  wr_hi, wr_lo, rb, expand=expand)
        else:
            w_in = mlstm_w_in[j]
            wg_hi, wg_lo = _hi_lo(_pad_lanes(w_in[:, 3 * d:]))
            gb = _pad_lanes(mlstm_gate_b[j].reshape(1, -1))
            qk, v, og, gates = _mlstm_in_proj(xf, gain, w_in[:, :3 * d].astype(BF16), wg_hi, wg_lo, gb)
            y = _mlstm_cell(qk, v, og, gates, mlstm_conv_w[j], mlstm_conv_b[j].reshape(1, d),
                            mlstm_h_norm[j].reshape(1, d), batch, seq)
            x_mid, h, route, counts = _post_mixer((y,), xf, mlstm_w_out[j].astype(BF16), ffn_gain,
                                                  wr_hi, wr_lo, rb)
        xf = _moe(x_mid, h, route, counts, moe_w_gate, moe_w_up, moe_w_down, layer)
    return xf.reshape(batch, seq, d)
```

```python
import functools

import jax
import jax.numpy as jnp
import numpy as np
from jax import lax
from jax.experimental import pallas as pl
from jax.experimental.pallas import tpu as pltpu

F32 = jnp.float32
BF16 = jnp.bfloat16

D_MODEL = 1024
N_MIXERS = 2
ATTN_HEAD_DIM = 64
ATTN_HEADS = D_MODEL // ATTN_HEAD_DIM
DILATIONS = (1, 4, 16)
ATTN_BLOCK = 128
ATTN_QBLOCKS = 4
MLSTM_HEADS = 4
MLSTM_V_DIM = D_MODEL // MLSTM_HEADS
MLSTM_QK_DIM = MLSTM_V_DIM // 2
MLSTM_CHUNK = 256
CONV_WIDTH = 4
N_GROUPS = 8
EXPERTS_PER_GROUP = 8
N_EXPERTS = N_GROUPS * EXPERTS_PER_GROUP
EXPERT_FF = D_MODEL // 2
MOE_BLOCK = 256
NORM_EPS = 1e-6

LANES = 128
ROW_TILE = 256
GATHER_TILE = 512
VMEM_LIMIT = 48 * 1024 * 1024
NEG = -1e30


def _dot(a, b):
    return jnp.dot(a, b, preferred_element_type=F32)


def _dot_nt(a, b):
    return lax.dot_general(a, b, (((1,), (1,)), ((), ())), preferred_element_type=F32)


def _rms(x, gain):
    ms = jnp.mean(x * x, axis=-1, keepdims=True)
    return x * lax.rsqrt(ms + NORM_EPS) * gain


def _hi_lo(x):
    hi = x.astype(BF16)
    lo = (x - hi.astype(F32)).astype(BF16)
    return hi, lo


def _dot3(x, w_hi, w_lo):
    x_hi, x_lo = _hi_lo(x)
    return _dot(x_hi, w_hi) + (_dot(x_lo, w_hi) + _dot(x_hi, w_lo))


U32 = jnp.uint32
ROW_SLABS = D_MODEL // (2 * LANES)
HIGH_HALF = np.uint32(0xFFFF0000)


def _store_rows(ref, x):
    half = D_MODEL // 2
    rows = x.shape[0]
    flat = ref.reshape(rows * ROW_SLABS, LANES)
    for j in range(ROW_SLABS):
        lo = pltpu.bitcast(x[:, j * LANES:(j + 1) * LANES].astype(BF16).astype(F32), U32) >> 16
        hi = pltpu.bitcast(x[:, half + j * LANES:half + (j + 1) * LANES].astype(BF16).astype(F32), U32) & HIGH_HALF
        flat[pl.ds(j, rows, stride=ROW_SLABS), :] = lo | hi


def _load_rows(ref):
    rows = ref.shape[0]
    flat = ref.reshape(rows * ROW_SLABS, LANES)
    words = [flat[pl.ds(j, rows, stride=ROW_SLABS), :] for j in range(ROW_SLABS)]
    lo = [pltpu.bitcast(w << 16, F32) for w in words]
    hi = [pltpu.bitcast(w & HIGH_HALF, F32) for w in words]
    return jnp.concatenate(lo + hi, axis=1)


def _resident(shape):
    nd = len(shape)
    return pl.BlockSpec(shape, lambda *_: (0,) * nd, pipeline_mode=pl.Buffered(1))


def _params(*sem):
    return pltpu.CompilerParams(dimension_semantics=sem, vmem_limit_bytes=VMEM_LIMIT)


def _dilation_perm(tm, dil):
    i = jnp.arange(tm)
    src = (i % (tm // dil)) * dil + i // (tm // dil)
    return (src[:, None] == jnp.arange(tm)[None, :]).astype(BF16)


def _attn_in_kernel(x_ref, g_ref, w_ref, bd_ref, qkg_ref, *rest):
    nd = len(DILATIONS)
    perm_refs, o_refs = rest[:nd - 1], rest[nd - 1:]
    tm = x_ref.shape[0]
    h = _rms(x_ref[...], g_ref[...]).astype(BF16)
    for c in range(3):
        cols = slice(c * D_MODEL, (c + 1) * D_MODEL)
        p = _dot(h, w_ref[:, cols])
        if c < 2:
            ss = _dot((p * p).astype(BF16), bd_ref[...])
            p = p * lax.rsqrt(ss * (1.0 / ATTN_HEAD_DIM) + NORM_EPS) * qkg_ref[:, cols]
        pb = p.astype(BF16)
        o_refs[0][:, cols] = pb
        for o_ref, perm_ref, dil in zip(o_refs[1:], perm_refs, DILATIONS[1:]):
            pp = _dot(perm_ref[...], pb).astype(BF16)
            for r in range(dil):
                dst = (3 * r + c) * D_MODEL
                o_ref[:, dst:dst + D_MODEL] = pp[r * (tm // dil):(r + 1) * (tm // dil), :]


def _attn_in_proj(x, gain, w, bd, qkg):
    n = x.shape[0]
    tm = ROW_TILE
    assert DILATIONS[0] == 1
    perms = [_dilation_perm(tm, dil) for dil in DILATIONS[1:]]
    return pl.pallas_call(
        _attn_in_kernel,
        grid=(n // tm,),
        in_specs=[
            pl.BlockSpec((tm, D_MODEL), lambda i: (i, 0)),
            _resident((1, D_MODEL)),
            _resident((D_MODEL, 3 * D_MODEL)),
            _resident((D_MODEL, D_MODEL)),
            _resident((1, 2 * D_MODEL)),
        ] + [_resident((tm, tm))] * len(perms),
        out_specs=[pl.BlockSpec((tm // dil, dil * 3 * D_MODEL), lambda i: (i, 0)) for dil in DILATIONS],
        out_shape=[jax.ShapeDtypeStruct((n // dil, dil * 3 * D_MODEL), BF16) for dil in DILATIONS],
        compiler_params=_params("parallel"),
        name="attn_in_proj",
    )(x, gain, w, bd, qkg, *perms)


def _attn_kernel(q_ref, kp_ref, kc_ref, vp_ref, vc_ref, o_ref, lse_ref, s_sc, m_sc):
    blk = ATTN_BLOCK
    for qb in range(ATTN_QBLOCKS):
        cur = slice(qb * blk, (qb + 1) * blk)
        if qb == 0:
            prev_k, prev_v, has_prev = kp_ref, vp_ref, pl.program_id(2) > 0
        else:
            prev = slice((qb - 1) * blk, qb * blk)
            prev_k, prev_v, has_prev = kc_ref.at[prev], vc_ref.at[prev], True
        _attn_block(q_ref.at[cur], prev_k, kc_ref.at[cur], prev_v, vc_ref.at[cur], has_prev,
                    o_ref.at[cur], lse_ref.at[cur], s_sc, m_sc)


def _attn_block(q_ref, kp_ref, kc_ref, vp_ref, vc_ref, has_prev, o_ref, lse_ref, s_sc, m_sc):
    blk = ATTN_BLOCK
    qi = lax.broadcasted_iota(jnp.int32, (blk, blk), 0)
    ki = lax.broadcasted_iota(jnp.int32, (blk, blk), 1)
    bias = jnp.concatenate([jnp.where((ki >= qi) & has_prev, 0.0, NEG), jnp.where(ki <= qi, 0.0, NEG)], axis=1)
    lane = lax.broadcasted_iota(jnp.int32, (1, LANES), 1)
    first = lane < ATTN_HEAD_DIM
    ones = jnp.ones((2 * blk, LANES), BF16)

    for pr in range(ATTN_HEADS // 2):
        cols = slice(pr * LANES, (pr + 1) * LANES)
        q2 = q_ref[:, cols]
        k_band = jnp.concatenate([kp_ref[:, cols], kc_ref[:, cols]], axis=0)
        for hh in range(2):
            h = 2 * pr + hh
            mine = first if hh == 0 else jnp.logical_not(first)
            qm = jnp.where(mine, q2, jnp.zeros_like(q2))
            s = _dot_nt(qm, k_band) + bias
            s_sc[h] = s
            m_sc[h] = jnp.broadcast_to(jnp.max(jnp.maximum(s[:, :blk], s[:, blk:]), axis=-1, keepdims=True),
                                       (blk, LANES))

    m_acc = jnp.zeros((blk, LANES), F32)
    l_acc = jnp.ones((blk, LANES), F32)
    for pr in range(ATTN_HEADS // 2):
        cols = slice(pr * LANES, (pr + 1) * LANES)
        v_band = jnp.concatenate([vp_ref[:, cols], vc_ref[:, cols]], axis=0)
        v_ones = jnp.concatenate([v_band, ones], axis=1)
        o_pair = None
        for hh in range(2):
            h = 2 * pr + hh
            m = m_sc[h]
            p = jnp.concatenate([jnp.exp(s_sc[h, :, 0:blk] - m).astype(BF16),
                                 jnp.exp(s_sc[h, :, blk:2 * blk] - m).astype(BF16)], axis=1)
            ol = _dot(p, v_ones)
            l = ol[:, LANES:]
            o2 = ol[:, :LANES] * (1.0 / l)
            o_pair = o2 if hh == 0 else jnp.where(first, o_pair, o2)
            m_acc = jnp.where(lane == h, m, m_acc)
            l_acc = jnp.where(lane == h, l, l_acc)
        o_ref[:, cols] = o_pair.astype(BF16)
    lse_ref[...] = jnp.where(lane < ATTN_HEADS, m_acc + jnp.log(l_acc), 0.0)


def _attn_branch(qkv_d, batch, seq, dil):
    blk = ATTN_BLOCK
    qb = ATTN_QBLOCKS
    rows = seq // dil
    steps = rows // (qb * blk)
    assert rows % (qb * blk) == 0
    qkv_v = qkv_d.reshape(batch, rows, dil * 3 * D_MODEL)

    def spec(c, prev):
        if prev:
            return pl.BlockSpec((None, blk, D_MODEL), lambda b, r, n: (b, jnp.maximum(qb * n - 1, 0), 3 * r + c))
        return pl.BlockSpec((None, qb * blk, D_MODEL), lambda b, r, n: (b, n, 3 * r + c))

    o, lse = pl.pallas_call(
        _attn_kernel,
        grid=(batch, dil, steps),
        in_specs=[spec(0, False), spec(1, True), spec(1, False), spec(2, True), spec(2, False)],
        out_specs=[
            pl.BlockSpec((None, qb * blk, D_MODEL), lambda b, r, n: (b, n, r)),
            pl.BlockSpec((None, qb * blk, LANES), lambda b, r, n: (b, n, r)),
        ],
        out_shape=[
            jax.ShapeDtypeStruct((batch, rows, dil * D_MODEL), BF16),
            jax.ShapeDtypeStruct((batch, rows, dil * LANES), F32),
        ],
        scratch_shapes=[
            pltpu.VMEM((ATTN_HEADS, blk, 2 * blk), F32),
            pltpu.VMEM((ATTN_HEADS, blk, LANES), F32),
        ],
        compiler_params=_params("parallel", "parallel", "arbitrary"),
        name=f"attn_dil{dil}",
    )(qkv_v, qkv_v, qkv_v, qkv_v, qkv_v)
    return o.reshape(batch * rows, dil * D_MODEL), lse.reshape(batch * rows, dil * LANES)


def _post_kernel(is_attn, *refs):
    if is_attn:
        o_refs, l_refs, e_ref, unperm_refs = refs[0:3], refs[3:6], refs[6], refs[7:9]
        refs = refs[9:]
    else:
        y_ref = refs[0]
        refs = refs[1:]
    (wout_ref, x_ref, gain_ref, wr_hi_ref, wr_lo_ref, rb_ref, tri_ref,
     xo_ref, h_ref, route_ref, cnt_ref, base_ref) = refs[:12]

    if is_attn:
        l_sc = refs[12]
        tm = x_ref.shape[0]
        for g, dil in enumerate(DILATIONS):
            for r in range(dil):
                l_sc[g, pl.ds(r, tm // dil, stride=dil), :] = l_refs[g][:, r * LANES:(r + 1) * LANES]
        ls = [l_sc[g] for g in range(3)]
        mx = jnp.maximum(jnp.maximum(ls[0], ls[1]), ls[2])
        es = [jnp.exp(l - mx) for l in ls]
        inv = 1.0 / (es[0] + es[1] + es[2])
        y = None
        for g, dil in enumerate(DILATIONS):
            w_hi, w_lo = _hi_lo(es[g] * inv)
            w_full = _dot(w_hi, e_ref[...]) + _dot(w_lo, e_ref[...])
            if dil == 1:
                o_g = o_refs[g][...].astype(F32)
            else:
                o_rm = jnp.concatenate([o_refs[g][:, r * D_MODEL:(r + 1) * D_MODEL] for r in range(dil)], axis=0)
                o_g = _dot(unperm_refs[g - 1][...], o_rm)
            t = w_full * o_g
            y = t if y is None else y + t
        y = y.astype(BF16)
    else:
        y = y_ref[...]

    xo = x_ref[...] + _dot(y, wout_ref[...])
    xo_ref[...] = xo
    hf = _rms(xo, gain_ref[...])
    _store_rows(h_ref, hf)

    logits = _dot3(hf, wr_hi_ref[...], wr_lo_ref[...]) + rb_ref[...]
    lane = lax.broadcasted_iota(jnp.int32, (1, LANES), 1)
    lanef = lane.astype(F32)
    is_grp = lane < N_GROUPS
    gl = jnp.where(is_grp, logits, NEG)
    gmax = jnp.max(gl, axis=-1, keepdims=True)
    gidx = jnp.min(jnp.where(gl == gmax, lanef, float(LANES)), axis=-1, keepdims=True)
    p_grp = 1.0 / jnp.sum(jnp.where(is_grp, jnp.exp(gl - gmax), 0.0), axis=-1, keepdims=True)
    lane_grp = ((lane - N_GROUPS) >> 3).astype(F32)
    in_grp = (lane >= N_GROUPS) & (lane < N_GROUPS + N_EXPERTS) & (lane_grp == gidx)
    em = jnp.where(in_grp, logits, NEG)
    v1 = jnp.max(em, axis=-1, keepdims=True)
    i1 = jnp.min(jnp.where(em == v1, lanef, float(LANES)), axis=-1, keepdims=True)
    hot0 = lanef == i1
    em2 = jnp.where(hot0, NEG, em)
    v2 = jnp.max(em2, axis=-1, keepdims=True)
    i2 = jnp.min(jnp.where(em2 == v2, lanef, float(LANES)), axis=-1, keepdims=True)
    hot1 = lanef == i2
    t2 = jnp.exp(v2 - v1)
    g0 = p_grp / (1.0 + t2)
    g1 = g0 * t2

    @pl.when(pl.program_id(0) == 0)
    def _():
        base_ref[...] = jnp.zeros_like(base_ref)

    hot = jnp.where(hot0 | hot1, 1.0, 0.0)
    before = _dot(tri_ref[...], hot.astype(BF16)) + base_ref[0:1, :]
    rank0 = jnp.sum(jnp.where(hot0, before, 0.0), axis=-1, keepdims=True)
    rank1 = jnp.sum(jnp.where(hot1, before, 0.0), axis=-1, keepdims=True)
    total = base_ref[0:1, :] + jnp.sum(hot, axis=0, keepdims=True)
    base_ref[...] = jnp.broadcast_to(total, base_ref.shape)
    cnt_ref[...] = jnp.broadcast_to(total, cnt_ref.shape)

    route = jnp.where(lane == 0, i1 - N_GROUPS, 0.0)
    route = jnp.where(lane == 1, i2 - N_GROUPS, route)
    route = jnp.where(lane == 2, rank0, route)
    route = jnp.where(lane == 3, rank1, route)
    route = jnp.where(lane == 4, g0, route)
    route = jnp.where(lane == 5, g1, route)
    route_ref[...] = route


def _post_mixer(mix, x, wout, gain, wr_hi, wr_lo, rb, expand=None):
    n = x.shape[0]
    tm = ROW_TILE
    is_attn = expand is not None
    row = lambda w: pl.BlockSpec((tm, w), lambda i: (i, 0))
    tri = (lax.broadcasted_iota(jnp.int32, (tm, tm), 0) > lax.broadcasted_iota(jnp.int32, (tm, tm), 1)).astype(BF16)
    if is_attn:
        dil_rows = lambda w: [pl.BlockSpec((tm // dil, dil * w), lambda i: (i, 0)) for dil in DILATIONS]
        unperms = [_dilation_perm(tm, dil).T for dil in DILATIONS[1:]]
        mix_specs = (dil_rows(D_MODEL) + dil_rows(LANES) + [_resident((LANES, D_MODEL))]
                     + [_resident((tm, tm))] * len(unperms))
        mix_args = tuple(mix) + (expand,) + tuple(unperms)
        scratch = [pltpu.VMEM((3, tm, LANES), F32)]
    else:
        mix_specs = [row(D_MODEL)]
        mix_args = tuple(mix)
        scratch = []
    return pl.pallas_call(
        functools.partial(_post_kernel, is_attn),
        grid=(n // tm,),
        in_specs=mix_specs + [
            _resident((D_MODEL, D_MODEL)),
            row(D_MODEL),
            _resident((1, D_MODEL)),
            _resident((D_MODEL, LANES)),
            _resident((D_MODEL, LANES)),
            _resident((1, LANES)),
            _resident((tm, tm)),
        ],
        out_specs=[row(D_MODEL), pl.BlockSpec((tm, ROW_SLABS, LANES), lambda i: (i, 0, 0)), row(LANES),
                   pl.BlockSpec((8, LANES), lambda i: (0, 0))],
        out_shape=[
            jax.ShapeDtypeStruct((n, D_MODEL), F32),
            jax.ShapeDtypeStruct((n, ROW_SLABS, LANES), U32),
            jax.ShapeDtypeStruct((n, LANES), F32),
            jax.ShapeDtypeStruct((8, LANES), F32),
        ],
        scratch_shapes=[pltpu.VMEM((8, LANES), F32)] + scratch,
        compiler_params=_params("arbitrary"),
        name="post_attn" if is_attn else "post_mlstm",
    )(*mix_args, wout, x, gain, wr_hi, wr_lo, rb, tri)


PAD_CHUNKS = tuple(1 << b for b in reversed(range(MOE_BLOCK.bit_length() - 1)))


def _dispatch_kernel(dest_ref, pad_row_ref, pad_len_ref, nused_ref, h_ref, xs_hbm, zeros_sc, sem, zero_sem):
    tm = GATHER_TILE

    @pl.when(pl.program_id(0) == 0)
    def _():
        zeros_sc[...] = jnp.zeros_like(zeros_sc)
        block = MOE_BLOCK

        def tail_copy(b):
            return pltpu.make_async_copy(zeros_sc, xs_hbm.at[pl.ds(b * block, block)], zero_sem)

        def start_tail(b, c):
            tail_copy(b).start()
            return c

        def wait_tail(b, c):
            tail_copy(b).wait()
            return c

        lax.fori_loop(nused_ref[0], xs_hbm.shape[0] // block, start_tail, 0)
        lax.fori_loop(nused_ref[0], xs_hbm.shape[0] // block, wait_tail, 0)

        def for_each_run(e, action):
            row, length = pad_row_ref[e], pad_len_ref[e]
            for chunk in PAD_CHUNKS:
                used = length & chunk

                @pl.when(used != 0)
                def _(row=row, chunk=chunk):
                    action(pltpu.make_async_copy(zeros_sc.at[pl.ds(0, chunk)], xs_hbm.at[pl.ds(row, chunk)], zero_sem))
                row = row + used

        def start_runs(e, c):
            for_each_run(e, lambda cp: cp.start())
            return c

        def wait_runs(e, c):
            for_each_run(e, lambda cp: cp.wait())
            return c

        lax.fori_loop(0, N_EXPERTS, start_runs, 0)
        lax.fori_loop(0, N_EXPERTS, wait_runs, 0)

    def copy(t, k):
        return pltpu.make_async_copy(h_ref.at[t], xs_hbm.at[dest_ref[0, 0, 2 * t + k]], sem)

    def start(t, c):
        copy(t, 0).start()
        copy(t, 1).start()
        return c

    def wait(t, c):
        copy(t, 0).wait()
        copy(t, 1).wait()
        return c

    lax.fori_loop(0, tm, start, 0, unroll=4)
    lax.fori_loop(0, tm, wait, 0, unroll=4)


def _dispatch(h, dest, pad_row, pad_len, nused, p_rows):
    n = h.shape[0]
    tm = GATHER_TILE
    return pl.pallas_call(
        _dispatch_kernel,
        grid=(n // tm,),
        in_specs=[
            pl.BlockSpec((1, 1, 2 * tm), lambda i: (i, 0, 0), memory_space=pltpu.SMEM),
            pl.BlockSpec(memory_space=pltpu.SMEM),
            pl.BlockSpec(memory_space=pltpu.SMEM),
            pl.BlockSpec(memory_space=pltpu.SMEM),
            pl.BlockSpec((tm, ROW_SLABS, LANES), lambda i: (i, 0, 0)),
        ],
        out_specs=pl.BlockSpec(memory_space=pl.ANY),
        out_shape=jax.ShapeDtypeStruct((p_rows, ROW_SLABS, LANES), U32),
        scratch_shapes=[pltpu.VMEM((MOE_BLOCK, ROW_SLABS, LANES), U32),
                        pltpu.SemaphoreType.DMA(()), pltpu.SemaphoreType.DMA(())],
        compiler_params=_params("arbitrary"),
        name="moe_dispatch",
    )(dest.reshape(n // tm, 1, 2 * tm), pad_row, pad_len, nused, h)


def _expert_kernel(bexp_ref, nused_ref, x_ref, wg_ref, wu_ref, wd_ref, o_ref):
    del bexp_ref
    is_live = pl.program_id(0) < nused_ref[0]

    @pl.when(jnp.logical_not(is_live))
    def _():
        o_ref[...] = jnp.zeros_like(o_ref)

    @pl.when(is_live)
    def _():
        xb = _load_rows(x_ref).astype(BF16)
        g = _dot(xb, wg_ref[...].astype(BF16))
        u = _dot(xb, wu_ref[...].astype(BF16))
        a = (g * (1.0 / (1.0 + jnp.exp(-g))) * u).astype(BF16)
        _store_rows(o_ref, _dot(a, wd_ref[...].astype(BF16)))


def _experts(xs, block_exp, nused, w_gate, w_up, w_down, layer):
    nb = xs.shape[0] // MOE_BLOCK
    live = lambda i, be, nu: jnp.minimum(i, nu[0] - 1)
    wsel = lambda i, be, nu: (layer, be[live(i, be, nu)], 0, 0)
    return pl.pallas_call(
        _expert_kernel,
        grid_spec=pltpu.PrefetchScalarGridSpec(
            num_scalar_prefetch=2,
            grid=(nb,),
            in_specs=[
                pl.BlockSpec((MOE_BLOCK, ROW_SLABS, LANES), lambda i, be, nu: (live(i, be, nu), 0, 0)),
                pl.BlockSpec((None, None, D_MODEL, EXPERT_FF), wsel),
                pl.BlockSpec((None, None, D_MODEL, EXPERT_FF), wsel),
                pl.BlockSpec((None, None, EXPERT_FF, D_MODEL), wsel),
            ],
            out_specs=pl.BlockSpec((MOE_BLOCK, ROW_SLABS, LANES), lambda i, be, nu: (i, 0, 0)),
        ),
        out_shape=jax.ShapeDtypeStruct(xs.shape, U32),
        compiler_params=_params("arbitrary"),
        name="moe_experts",
    )(block_exp, nused, xs, w_gate, w_up, w_down)


def _combine_kernel(dcur_ref, dnext_ref, ys_hbm, x_ref, route_ref, o_ref, buf, sems):
    tm = GATHER_TILE
    i = pl.program_id(0)
    slot = i % 2

    def copy(d_ref, s, t, k):
        return pltpu.make_async_copy(ys_hbm.at[d_ref[0, 0, 2 * t + k]], buf.at[s, k, t], sems.at[s])

    def gather(d_ref, s):
        def start(t, c):
            copy(d_ref, s, t, 0).start()
            copy(d_ref, s, t, 1).start()
            return c
        lax.fori_loop(0, tm, start, 0, unroll=4)

    @pl.when(i == 0)
    def _():
        gather(dcur_ref, 0)

    @pl.when(i + 1 < pl.num_programs(0))
    def _():
        gather(dnext_ref, 1 - slot)

    def wait(t, c):
        copy(dcur_ref, slot, t, 0).wait()
        copy(dcur_ref, slot, t, 1).wait()
        return c

    lax.fori_loop(0, tm, wait, 0, unroll=4)
    r = route_ref[...]
    y0 = _load_rows(buf.at[slot, 0])
    y1 = _load_rows(buf.at[slot, 1])
    o_ref[...] = x_ref[...] + r[:, 4:5] * y0 + r[:, 5:6] * y1


def _combine(ys, dest, x, route):
    n = x.shape[0]
    tm = GATHER_TILE
    nt = n // tm
    dest_t = dest.reshape(nt, 1, 2 * tm)
    return pl.pallas_call(
        _combine_kernel,
        grid=(nt,),
        in_specs=[
            pl.BlockSpec((1, 1, 2 * tm), lambda i: (i, 0, 0), memory_space=pltpu.SMEM),
            pl.BlockSpec((1, 1, 2 * tm), lambda i: (jnp.minimum(i + 1, nt - 1), 0, 0), memory_space=pltpu.SMEM),
            pl.BlockSpec(memory_space=pl.ANY),
            pl.BlockSpec((tm, D_MODEL), lambda i: (i, 0)),
            pl.BlockSpec((tm, LANES), lambda i: (i, 0)),
        ],
        out_specs=pl.BlockSpec((tm, D_MODEL), lambda i: (i, 0)),
        out_shape=jax.ShapeDtypeStruct((n, D_MODEL), F32),
        scratch_shapes=[pltpu.VMEM((2, 2, tm, ROW_SLABS, LANES), U32), pltpu.SemaphoreType.DMA((2,))],
        compiler_params=_params("arbitrary"),
        name="moe_combine",
    )(dest_t, dest_t, ys, x, route)


def _moe(x_mid, h, route, counts, w_gate, w_up, w_down, layer):
    n = x_mid.shape[0]
    nb = (n * 2) // MOE_BLOCK + N_EXPERTS
    cnt = counts[0, N_GROUPS:N_GROUPS + N_EXPERTS].astype(jnp.int32)
    padded = ((cnt + MOE_BLOCK - 1) // MOE_BLOCK) * MOE_BLOCK
    pad_end = jnp.cumsum(padded)
    pad_start = pad_end - padded
    ri = route[:, 0:4].astype(jnp.int32)
    start_of = jnp.sum(jnp.where(ri[:, 0:2, None] == jnp.arange(N_EXPERTS, dtype=jnp.int32), pad_start, 0), axis=-1)
    dest = start_of + ri[:, 2:4]
    block_row = jnp.arange(nb, dtype=jnp.int32) * MOE_BLOCK
    block_exp = jnp.minimum(jnp.sum(pad_end[None, :] <= block_row[:, None], axis=1), N_EXPERTS - 1).astype(jnp.int32)
    nused = (pad_end[-1:] // MOE_BLOCK).astype(jnp.int32)
    xs = _dispatch(h, dest, pad_start + cnt, padded - cnt, nused, nb * MOE_BLOCK)
    ys = _experts(xs, block_exp, nused, w_gate, w_up, w_down, layer)
    return _combine(ys, dest, x_mid, route)


def _mlstm_in_kernel(x_ref, g_ref, w_ref, wg_hi_ref, wg_lo_ref, gb_ref, qk_ref, v_ref, og_ref, gt_ref):
    hf = _rms(x_ref[...], g_ref[...])
    h = hf.astype(BF16)
    qk_ref[...] = _dot(h, w_ref[:, 0:D_MODEL])
    v_ref[...] = _dot(h, w_ref[:, D_MODEL:2 * D_MODEL]).astype(BF16)
    og_ref[...] = _dot(h, w_ref[:, 2 * D_MODEL:3 * D_MODEL]).astype(BF16)
    gt_ref[...] = _dot3(hf, wg_hi_ref[...], wg_lo_ref[...]) + gb_ref[...]


def _mlstm_in_proj(x, gain, w, wg_hi, wg_lo, gb):
    n = x.shape[0]
    tm = ROW_TILE
    row = lambda w_: pl.BlockSpec((tm, w_), lambda i: (i, 0))
    return pl.pallas_call(
        _mlstm_in_kernel,
        grid=(n // tm,),
        in_specs=[
            row(D_MODEL),
            _resident((1, D_MODEL)),
            _resident((D_MODEL, 3 * D_MODEL)),
            _resident((D_MODEL, LANES)),
            _resident((D_MODEL, LANES)),
            _resident((1, LANES)),
        ],
        out_specs=[row(D_MODEL), row(D_MODEL), row(D_MODEL), row(LANES)],
        out_shape=[
            jax.ShapeDtypeStruct((n, D_MODEL), F32),
            jax.ShapeDtypeStruct((n, D_MODEL), BF16),
            jax.ShapeDtypeStruct((n, D_MODEL), BF16),
            jax.ShapeDtypeStruct((n, LANES), F32),
        ],
        compiler_params=_params("parallel"),
        name="mlstm_in_proj",
    )(x, gain, w, wg_hi, wg_lo, gb)


def _log_sigmoid(x):
    return jnp.minimum(x, 0.0) - jnp.log(1.0 + jnp.exp(-jnp.abs(x)))


def _mlstm_kernel(qk_ref, v_ref, og_ref, gt_ref, cw_ref, cb_ref, hg_ref, tri_ref, y_ref,
                  cbuf, c_sc, n_sc, m_sc):
    L = MLSTM_CHUNK
    dk, dv = MLSTM_QK_DIM, MLSTM_V_DIM

    @pl.when(pl.program_id(1) == 0)
    def _():
        cbuf[0:8, :] = jnp.zeros((8, D_MODEL), F32)
        c_sc[...] = jnp.zeros_like(c_sc)
        n_sc[...] = jnp.zeros_like(n_sc)
        m_sc[...] = jnp.zeros_like(m_sc)

    cbuf[8:8 + L, :] = qk_ref[...]
    conv = cb_ref[...] + cw_ref[0:1, :] * cbuf[5:5 + L, :]
    for j in range(1, CONV_WIDTH):
        conv = conv + cw_ref[j:j + 1, :] * cbuf[5 + j:5 + j + L, :]
    cbuf[0:8, :] = cbuf[L:L + 8, :]
    qk = conv * (1.0 / (1.0 + jnp.exp(-conv)))

    gates = gt_ref[...]
    lf = _log_sigmoid(gates)
    lf_hi = lf.astype(BF16)
    lf_mid, lf_lo = _hi_lo(lf - lf_hi.astype(F32))
    tri = tri_ref[...]
    b = _dot(tri, lf_hi) + (_dot(tri, lf_mid) + _dot(tri, lf_lo))
    gates_t = gates.T
    b_t = b.T
    ti = lax.broadcasted_iota(jnp.int32, (L, L), 0)
    si = lax.broadcasted_iota(jnp.int32, (L, L), 1)
    causal = ti >= si

    for h in range(MLSTM_HEADS):
        i_col, i_row = gates[:, h:h + 1], gates_t[h:h + 1, :]
        b_col, b_row = b[:, 4 + h:5 + h], b_t[4 + h:5 + h, :]
        m_prev = m_sc[h, 0:1, 0:1]
        q = qk[:, h * dk:(h + 1) * dk]
        k = qk[:, D_MODEL // 2 + h * dk:D_MODEL // 2 + (h + 1) * dk] * (dk ** -0.5)
        qb, kb = q.astype(BF16), k.astype(BF16)
        v = v_ref[:, h * dv:(h + 1) * dv]

        dmat = jnp.where(causal, b_col - b_row + i_row, NEG)
        m_inter = b_col + m_prev
        m_t = jnp.maximum(m_inter, jnp.max(dmat, axis=-1, keepdims=True))
        s = _dot_nt(qb, kb) * jnp.exp(dmat - m_t)
        w_inter = jnp.exp(m_inter - m_t)
        num = _dot(s.astype(BF16), v) + w_inter * _dot(qb, c_sc[h].astype(BF16))
        den = jnp.sum(s, axis=-1, keepdims=True) + w_inter * jnp.sum(q * n_sc[h], axis=-1, keepdims=True)
        hh = num / jnp.maximum(jnp.abs(den), jnp.exp(-m_t))

        b_last = b_col[L - 1:L, :]
        g = b_last - b_col + i_col
        m_new = jnp.maximum(b_last + m_prev, jnp.max(g, axis=0, keepdims=True))
        decay = jnp.exp(b_last + m_prev - m_new)
        wk = jnp.exp(g - m_new) * k
        c_sc[h] = decay * c_sc[h] + _dot(wk.T.astype(BF16), v)
        n_sc[h] = decay * n_sc[h] + jnp.sum(wk, axis=0, keepdims=True)
        m_sc[h] = jnp.broadcast_to(m_new, (8, LANES))

        cols = slice(h * dv, (h + 1) * dv)
        hn = hh * lax.rsqrt(jnp.mean(hh * hh, axis=-1, keepdims=True) + NORM_EPS) * hg_ref[:, cols]
        og = og_ref[:, cols].astype(F32)
        y_ref[:, cols] = (hn * (1.0 / (1.0 + jnp.exp(-og)))).astype(BF16)


def _mlstm_cell(qk, v, og, gates, conv_w, conv_b, h_gain, batch, seq):
    L = MLSTM_CHUNK
    nc = seq // L
    row = lambda w_: pl.BlockSpec((L, w_), lambda b, c: (b * nc + c, 0))
    tri = (lax.broadcasted_iota(jnp.int32, (L, L), 0) >= lax.broadcasted_iota(jnp.int32, (L, L), 1)).astype(BF16)
    return pl.pallas_call(
        _mlstm_kernel,
        grid=(batch, nc),
        in_specs=[
            row(D_MODEL), row(D_MODEL), row(D_MODEL), row(LANES),
            _resident((CONV_WIDTH, D_MODEL)),
            _resident((1, D_MODEL)),
            _resident((1, D_MODEL)),
            _resident((L, L)),
        ],
        out_specs=row(D_MODEL),
        out_shape=jax.ShapeDtypeStruct((batch * seq, D_MODEL), BF16),
        scratch_shapes=[
            pltpu.VMEM((L + 8, D_MODEL), F32),
            pltpu.VMEM((MLSTM_HEADS, MLSTM_QK_DIM, MLSTM_V_DIM), F32),
            pltpu.VMEM((MLSTM_HEADS, 1, MLSTM_QK_DIM), F32),
            pltpu.VMEM((MLSTM_HEADS, 8, LANES), F32),
        ],
        compiler_params=_params("arbitrary", "arbitrary"),
        name="mlstm_cell",
    )(qk, v, og, gates, conv_w, conv_b, h_gain, tri)


def _pad_lanes(w):
    return jnp.pad(w, ((0, 0), (0, LANES - w.shape[1])))


def kernel(x, norm_mix, norm_ffn, attn_w_in, attn_q_norm, attn_k_norm, attn_w_out, mlstm_w_in, mlstm_conv_w, mlstm_conv_b, mlstm_gate_b, mlstm_h_norm, mlstm_w_out, moe_w_group, moe_b_group, moe_w_expert, moe_b_expert, moe_w_gate, moe_w_up, moe_w_down):
    batch, seq, d = x.shape
    assert d == D_MODEL and seq % (ATTN_BLOCK * max(DILATIONS)) == 0 and seq % MLSTM_CHUNK == 0
    assert (batch * seq) % GATHER_TILE == 0 and GATHER_TILE % ROW_TILE == 0
    depth = norm_mix.shape[0]
    n = batch * seq
    xf = x.reshape(n, d)

    head_of = jnp.arange(D_MODEL) // ATTN_HEAD_DIM
    bd = (head_of[:, None] == head_of[None, :]).astype(BF16)
    expand = (jnp.arange(LANES)[:, None] == head_of[None, :]).astype(BF16)

    for layer in range(depth):
        j = layer // N_MIXERS
        gain = norm_mix[layer].reshape(1, d)
        wr = jnp.concatenate([moe_w_group[layer], moe_w_expert[layer]], axis=1)
        wr_hi, wr_lo = _hi_lo(_pad_lanes(wr))
        rb = _pad_lanes(jnp.concatenate([moe_b_group[layer], moe_b_expert[layer]]).reshape(1, -1))
        ffn_gain = norm_ffn[layer].reshape(1, d)
        if layer % N_MIXERS == 0:
            qkg = jnp.concatenate([jnp.tile(attn_q_norm[j], ATTN_HEADS) * ATTN_HEAD_DIM ** -0.5,
                                   jnp.tile(attn_k_norm[j], ATTN_HEADS)]).reshape(1, 2 * d)
            qkvs = _attn_in_proj(xf, gain, attn_w_in[j].astype(BF16), bd, qkg)
            outs = [_attn_branch(qkv_d, batch, seq, dil) for qkv_d, dil in zip(qkvs, DILATIONS)]
            mix = tuple(o for o, _ in outs) + tuple(l for _, l in outs)
            x_mid, h, route, counts = _post_mixer(mix, xf, attn_w_out[j].astype(BF16), ffn_gain,
                                                  wr_hi, wr_lo, rb, expand=expand)
        else:
            w_in = mlstm_w_in[j]
            wg_hi, wg_lo = _hi_lo(_pad_lanes(w_in[:, 3 * d:]))
            gb = _pad_lanes(mlstm_gate_b[j].reshape(1, -1))
            qk, v, og, gates = _mlstm_in_proj(xf, gain, w_in[:, :3 * d].astype(BF16), wg_hi, wg_lo, gb)
            y = _mlstm_cell(qk, v, og, gates, mlstm_conv_w[j], mlstm_conv_b[j].reshape(1, d),
                            mlstm_h_norm[j].reshape(1, d), batch, seq)
            x_mid, h, route, counts = _post_mixer((y,), xf, mlstm_w_out[j].astype(BF16), ffn_gain,
                                                  wr_hi, wr_lo, rb)
        xf = _moe(x_mid, h, route, counts, moe_w_gate, moe_w_up, moe_w_down, layer)
    return xf.reshape(batch, seq, d)
```

```python
import functools

import jax
import jax.numpy as jnp
import numpy as np
from jax import lax
from jax.experimental import pallas as pl
from jax.experimental.pallas import tpu as pltpu

F32 = jnp.float32
BF16 = jnp.bfloat16

D_MODEL = 1024
N_MIXERS = 2
ATTN_HEAD_DIM = 64
ATTN_HEADS = D_MODEL // ATTN_HEAD_DIM
DILATIONS = (1, 4, 16)
ATTN_BLOCK = 128
ATTN_QBLOCKS = 4
MLSTM_HEADS = 4
MLSTM_V_DIM = D_MODEL // MLSTM_HEADS
MLSTM_QK_DIM = MLSTM_V_DIM // 2
MLSTM_CHUNK = 256
CONV_WIDTH = 4
N_GROUPS = 8
EXPERTS_PER_GROUP = 8
N_EXPERTS = N_GROUPS * EXPERTS_PER_GROUP
EXPERT_FF = D_MODEL // 2
MOE_BLOCK = 256
NORM_EPS = 1e-6

LANES = 128
ROW_TILE = 256
WIDE_ROW_TILE = 512
GATHER_TILE = 1024
VMEM_LIMIT = 48 * 1024 * 1024
NEG = -1e30


def _dot(a, b):
    return jnp.dot(a, b, preferred_element_type=F32)


def _dot_nt(a, b):
    return lax.dot_general(a, b, (((1,), (1,)), ((), ())), preferred_element_type=F32)


def _rms(x, gain):
    ms = jnp.mean(x * x, axis=-1, keepdims=True)
    return x * lax.rsqrt(ms + NORM_EPS) * gain


def _hi_lo(x):
    hi = x.astype(BF16)
    lo = (x - hi.astype(F32)).astype(BF16)
    return hi, lo


def _dot3(x, w_hi, w_lo):
    x_hi, x_lo = _hi_lo(x)
    return _dot(x_hi, w_hi) + (_dot(x_lo, w_hi) + _dot(x_hi, w_lo))


U32 = jnp.uint32
ROW_SLABS = D_MODEL // (2 * LANES)
HIGH_HALF = np.uint32(0xFFFF0000)


def _store_rows(ref, x):
    half = D_MODEL // 2
    rows = x.shape[0]
    flat = ref.reshape(rows * ROW_SLABS, LANES)
    for j in range(ROW_SLABS):
        lo = pltpu.bitcast(x[:, j * LANES:(j + 1) * LANES].astype(BF16).astype(F32), U32) >> 16
        hi = pltpu.bitcast(x[:, half + j * LANES:half + (j + 1) * LANES].astype(BF16).astype(F32), U32) & HIGH_HALF
        flat[pl.ds(j, rows, stride=ROW_SLABS), :] = lo | hi


def _load_rows(ref):
    rows = ref.shape[0]
    flat = ref.reshape(rows * ROW_SLABS, LANES)
    words = [flat[pl.ds(j, rows, stride=ROW_SLABS), :] for j in range(ROW_SLABS)]
    lo = [pltpu.bitcast(w << 16, F32) for w in words]
    hi = [pltpu.bitcast(w & HIGH_HALF, F32) for w in words]
    return jnp.concatenate(lo + hi, axis=1)


def _resident(shape):
    nd = len(shape)
    return pl.BlockSpec(shape, lambda *_: (0,) * nd, pipeline_mode=pl.Buffered(1))


def _params(*sem):
    return pltpu.CompilerParams(dimension_semantics=sem, vmem_limit_bytes=VMEM_LIMIT)


def _dilation_perm(tm, dil):
    i = jnp.arange(tm)
    src = (i % (tm // dil)) * dil + i // (tm // dil)
    return (src[:, None] == jnp.arange(tm)[None, :]).astype(BF16)


def _attn_in_kernel(x_ref, g_ref, w_ref, bd_ref, qkg_ref, *rest):
    nd = len(DILATIONS)
    perm_refs, o_refs = rest[:nd - 1], rest[nd - 1:]
    tm = x_ref.shape[0]
    h = _rms(x_ref[...], g_ref[...]).astype(BF16)
    for c in range(3):
        cols = slice(c * D_MODEL, (c + 1) * D_MODEL)
        p = _dot(h, w_ref[:, cols])
        if c < 2:
            ss = _dot((p * p).astype(BF16), bd_ref[...])
            p = p * lax.rsqrt(ss * (1.0 / ATTN_HEAD_DIM) + NORM_EPS) * qkg_ref[:, cols]
        pb = p.astype(BF16)
        o_refs[0][:, cols] = pb
        for o_ref, perm_ref, dil in zip(o_refs[1:], perm_refs, DILATIONS[1:]):
            pp = _dot(perm_ref[...], pb).astype(BF16)
            for r in range(dil):
                dst = (3 * r + c) * D_MODEL
                o_ref[:, dst:dst + D_MODEL] = pp[r * (tm // dil):(r + 1) * (tm // dil), :]


def _attn_in_proj(x, gain, w, bd, qkg):
    n = x.shape[0]
    tm = ROW_TILE
    assert DILATIONS[0] == 1
    perms = [_dilation_perm(tm, dil) for dil in DILATIONS[1:]]
    return pl.pallas_call(
        _attn_in_kernel,
        grid=(n // tm,),
        in_specs=[
            pl.BlockSpec((tm, D_MODEL), lambda i: (i, 0)),
            _resident((1, D_MODEL)),
            _resident((D_MODEL, 3 * D_MODEL)),
            _resident((D_MODEL, D_MODEL)),
            _resident((1, 2 * D_MODEL)),
        ] + [_resident((tm, tm))] * len(perms),
        out_specs=[pl.BlockSpec((tm // dil, dil * 3 * D_MODEL), lambda i: (i, 0)) for dil in DILATIONS],
        out_shape=[jax.ShapeDtypeStruct((n // dil, dil * 3 * D_MODEL), BF16) for dil in DILATIONS],
        compiler_params=_params("parallel"),
        name="attn_in_proj",
    )(x, gain, w, bd, qkg, *perms)


def _attn_kernel(q_ref, kp_ref, kc_ref, vp_ref, vc_ref, o_ref, lse_ref, s_sc, m_sc):
    blk = ATTN_BLOCK
    for qb in range(ATTN_QBLOCKS):
        cur = slice(qb * blk, (qb + 1) * blk)
        if qb == 0:
            prev_k, prev_v, has_prev = kp_ref, vp_ref, pl.program_id(2) > 0
        else:
            prev = slice((qb - 1) * blk, qb * blk)
            prev_k, prev_v, has_prev = kc_ref.at[prev], vc_ref.at[prev], True
        _attn_block(q_ref.at[cur], prev_k, kc_ref.at[cur], prev_v, vc_ref.at[cur], has_prev,
                    o_ref.at[cur], lse_ref.at[cur], s_sc, m_sc)


def _attn_block(q_ref, kp_ref, kc_ref, vp_ref, vc_ref, has_prev, o_ref, lse_ref, s_sc, m_sc):
    blk = ATTN_BLOCK
    qi = lax.broadcasted_iota(jnp.int32, (blk, blk), 0)
    ki = lax.broadcasted_iota(jnp.int32, (blk, blk), 1)
    bias = jnp.concatenate([jnp.where((ki >= qi) & has_prev, 0.0, NEG), jnp.where(ki <= qi, 0.0, NEG)], axis=1)
    lane = lax.broadcasted_iota(jnp.int32, (1, LANES), 1)
    first = lane < ATTN_HEAD_DIM
    ones = jnp.ones((2 * blk, LANES), BF16)

    for pr in range(ATTN_HEADS // 2):
        cols = slice(pr * LANES, (pr + 1) * LANES)
        q2 = q_ref[:, cols]
        k_band = jnp.concatenate([kp_ref[:, cols], kc_ref[:, cols]], axis=0)
        for hh in range(2):
            h = 2 * pr + hh
            mine = first if hh == 0 else jnp.logical_not(first)
            qm = jnp.where(mine, q2, jnp.zeros_like(q2))
            s = _dot_nt(qm, k_band) + bias
            s_sc[h] = s
            m_sc[h] = jnp.broadcast_to(jnp.max(jnp.maximum(s[:, :blk], s[:, blk:]), axis=-1, keepdims=True),
                                       (blk, LANES))

    m_acc = jnp.zeros((blk, LANES), F32)
    l_acc = jnp.ones((blk, LANES), F32)
    for pr in range(ATTN_HEADS // 2):
        cols = slice(pr * LANES, (pr + 1) * LANES)
        v_band = jnp.concatenate([vp_ref[:, cols], vc_ref[:, cols]], axis=0)
        v_ones = jnp.concatenate([v_band, ones], axis=1)
        o_pair = None
        for hh in range(2):
            h = 2 * pr + hh
            m = m_sc[h]
            p = jnp.concatenate([jnp.exp(s_sc[h, :, 0:blk] - m).astype(BF16),
                                 jnp.exp(s_sc[h, :, blk:2 * blk] - m).astype(BF16)], axis=1)
            ol = _dot(p, v_ones)
            l = ol[:, LANES:]
            o2 = ol[:, :LANES] * (1.0 / l)
            o_pair = o2 if hh == 0 else jnp.where(first, o_pair, o2)
            m_acc = jnp.where(lane == h, m, m_acc)
            l_acc = jnp.where(lane == h, l, l_acc)
        o_ref[:, cols] = o_pair.astype(BF16)
    lse_ref[...] = jnp.where(lane < ATTN_HEADS, m_acc + jnp.log(l_acc), 0.0)


def _attn_branch(qkv_d, batch, seq, dil):
    blk = ATTN_BLOCK
    qb = ATTN_QBLOCKS
    rows = seq // dil
    steps = rows // (qb * blk)
    assert rows % (qb * blk) == 0
    qkv_v = qkv_d.reshape(batch, rows, dil * 3 * D_MODEL)

    def spec(c, prev):
        if prev:
            return pl.BlockSpec((None, blk, D_MODEL), lambda b, r, n: (b, jnp.maximum(qb * n - 1, 0), 3 * r + c))
        return pl.BlockSpec((None, qb * blk, D_MODEL), lambda b, r, n: (b, n, 3 * r + c))

    o, lse = pl.pallas_call(
        _attn_kernel,
        grid=(batch, dil, steps),
        in_specs=[spec(0, False), spec(1, True), spec(1, False), spec(2, True), spec(2, False)],
        out_specs=[
            pl.BlockSpec((None, qb * blk, D_MODEL), lambda b, r, n: (b, n, r)),
            pl.BlockSpec((None, qb * blk, LANES), lambda b, r, n: (b, n, r)),
        ],
        out_shape=[
            jax.ShapeDtypeStruct((batch, rows, dil * D_MODEL), BF16),
            jax.ShapeDtypeStruct((batch, rows, dil * LANES), F32),
        ],
        scratch_shapes=[
            pltpu.VMEM((ATTN_HEADS, blk, 2 * blk), F32),
            pltpu.VMEM((ATTN_HEADS, blk, LANES), F32),
        ],
        compiler_params=_params("parallel", "parallel", "arbitrary"),
        name=f"attn_dil{dil}",
    )(qkv_v, qkv_v, qkv_v, qkv_v, qkv_v)
    return o.reshape(batch * rows, dil * D_MODEL), lse.reshape(batch * rows, dil * LANES)


def _post_kernel(is_attn, *refs):
    if is_attn:
        o_refs, l_refs, e_ref, unperm_refs = refs[0:3], refs[3:6], refs[6], refs[7:9]
        refs = refs[9:]
    else:
        y_ref = refs[0]
        refs = refs[1:]
    (wout_ref, x_ref, gain_ref, wr_hi_ref, wr_lo_ref, rb_ref, tri_ref,
     xo_ref, h_ref, route_ref, cnt_ref, base_ref) = refs[:12]

    if is_attn:
        l_sc = refs[12]
        tm = x_ref.shape[0]
        for g, dil in enumerate(DILATIONS):
            for r in range(dil):
                l_sc[g, pl.ds(r, tm // dil, stride=dil), :] = l_refs[g][:, r * LANES:(r + 1) * LANES]
        ls = [l_sc[g] for g in range(3)]
        mx = jnp.maximum(jnp.maximum(ls[0], ls[1]), ls[2])
        es = [jnp.exp(l - mx) for l in ls]
        inv = 1.0 / (es[0] + es[1] + es[2])
        y = None
        for g, dil in enumerate(DILATIONS):
            w_hi, w_lo = _hi_lo(es[g] * inv)
            w_full = _dot(w_hi, e_ref[...]) + _dot(w_lo, e_ref[...])
            if dil == 1:
                o_g = o_refs[g][...].astype(F32)
            else:
                o_rm = jnp.concatenate([o_refs[g][:, r * D_MODEL:(r + 1) * D_MODEL] for r in range(dil)], axis=0)
                o_g = _dot(unperm_refs[g - 1][...], o_rm)
            t = w_full * o_g
            y = t if y is None else y + t
        y = y.astype(BF16)
    else:
        y = y_ref[...]

    xo = x_ref[...] + _dot(y, wout_ref[...])
    xo_ref[...] = xo
    hf = _rms(xo, gain_ref[...])
    _store_rows(h_ref, hf)

    logits = _dot3(hf, wr_hi_ref[...], wr_lo_ref[...]) + rb_ref[...]
    lane = lax.broadcasted_iota(jnp.int32, (1, LANES), 1)
    lanef = lane.astype(F32)
    is_grp = lane < N_GROUPS
    gl = jnp.where(is_grp, logits, NEG)
    gmax = jnp.max(gl, axis=-1, keepdims=True)
    gidx = jnp.min(jnp.where(gl == gmax, lanef, float(LANES)), axis=-1, keepdims=True)
    p_grp = 1.0 / jnp.sum(jnp.where(is_grp, jnp.exp(gl - gmax), 0.0), axis=-1, keepdims=True)
    lane_grp = ((lane - N_GROUPS) >> 3).astype(F32)
    in_grp = (lane >= N_GROUPS) & (lane < N_GROUPS + N_EXPERTS) & (lane_grp == gidx)
    em = jnp.where(in_grp, logits, NEG)
    v1 = jnp.max(em, axis=-1, keepdims=True)
    i1 = jnp.min(jnp.where(em == v1, lanef, float(LANES)), axis=-1, keepdims=True)
    hot0 = lanef == i1
    em2 = jnp.where(hot0, NEG, em)
    v2 = jnp.max(em2, axis=-1, keepdims=True)
    i2 = jnp.min(jnp.where(em2 == v2, lanef, float(LANES)), axis=-1, keepdims=True)
    hot1 = lanef == i2
    t2 = jnp.exp(v2 - v1)
    g0 = p_grp / (1.0 + t2)
    g1 = g0 * t2

    @pl.when(pl.program_id(0) == 0)
    def _():
        base_ref[...] = jnp.zeros_like(base_ref)

    hot = jnp.where(hot0 | hot1, 1.0, 0.0)
    before = _dot(tri_ref[...], hot.astype(BF16)) + base_ref[0:1, :]
    rank0 = jnp.sum(jnp.where(hot0, before, 0.0), axis=-1, keepdims=True)
    rank1 = jnp.sum(jnp.where(hot1, before, 0.0), axis=-1, keepdims=True)
    total = base_ref[0:1, :] + jnp.sum(hot, axis=0, keepdims=True)
    base_ref[...] = jnp.broadcast_to(total, base_ref.shape)
    cnt_ref[...] = jnp.broadcast_to(total, cnt_ref.shape)

    route = jnp.where(lane == 0, i1 - N_GROUPS, 0.0)
    route = jnp.where(lane == 1, i2 - N_GROUPS, route)
    route = jnp.where(lane == 2, rank0, route)
    route = jnp.where(lane == 3, rank1, route)
    route = jnp.where(lane == 4, g0, route)
    route = jnp.where(lane == 5, g1, route)
    route_ref[...] = route


def _post_mixer(mix, x, wout, gain, wr_hi, wr_lo, rb, expand=None):
    n = x.shape[0]
    is_attn = expand is not None
    tm = ROW_TILE if is_attn else WIDE_ROW_TILE
    row = lambda w: pl.BlockSpec((tm, w), lambda i: (i, 0))
    tri = (lax.broadcasted_iota(jnp.int32, (tm, tm), 0) > lax.broadcasted_iota(jnp.int32, (tm, tm), 1)).astype(BF16)
    if is_attn:
        dil_rows = lambda w: [pl.BlockSpec((tm // dil, dil * w), lambda i: (i, 0)) for dil in DILATIONS]
        unperms = [_dilation_perm(tm, dil).T for dil in DILATIONS[1:]]
        mix_specs = (dil_rows(D_MODEL) + dil_rows(LANES) + [_resident((LANES, D_MODEL))]
                     + [_resident((tm, tm))] * len(unperms))
        mix_args = tuple(mix) + (expand,) + tuple(unperms)
        scratch = [pltpu.VMEM((3, tm, LANES), F32)]
    else:
        mix_specs = [row(D_MODEL)]
        mix_args = tuple(mix)
        scratch = []
    return pl.pallas_call(
        functools.partial(_post_kernel, is_attn),
        grid=(n // tm,),
        in_specs=mix_specs + [
            _resident((D_MODEL, D_MODEL)),
            row(D_MODEL),
            _resident((1, D_MODEL)),
            _resident((D_MODEL, LANES)),
            _resident((D_MODEL, LANES)),
            _resident((1, LANES)),
            _resident((tm, tm)),
        ],
        out_specs=[row(D_MODEL), pl.BlockSpec((tm, ROW_SLABS, LANES), lambda i: (i, 0, 0)), row(LANES),
                   pl.BlockSpec((8, LANES), lambda i: (0, 0))],
        out_shape=[
            jax.ShapeDtypeStruct((n, D_MODEL), F32),
            jax.ShapeDtypeStruct((n, ROW_SLABS, LANES), U32),
            jax.ShapeDtypeStruct((n, LANES), F32),
            jax.ShapeDtypeStruct((8, LANES), F32),
        ],
        scratch_shapes=[pltpu.VMEM((8, LANES), F32)] + scratch,
        compiler_params=_params("arbitrary"),
        name="post_attn" if is_attn else "post_mlstm",
    )(*mix_args, wout, x, gain, wr_hi, wr_lo, rb, tri)


PAD_CHUNKS = tuple(1 << b for b in reversed(range(MOE_BLOCK.bit_length() - 1)))


def _dispatch_kernel(dest_ref, pad_row_ref, pad_len_ref, nused_ref, h_ref, xs_hbm, zeros_sc, sem, zero_sem):
    tm = GATHER_TILE

    @pl.when(pl.program_id(0) == 0)
    def _():
        zeros_sc[...] = jnp.zeros_like(zeros_sc)
        block = MOE_BLOCK

        def tail_copy(b):
            return pltpu.make_async_copy(zeros_sc, xs_hbm.at[pl.ds(b * block, block)], zero_sem)

        def start_tail(b, c):
            tail_copy(b).start()
            return c

        def wait_tail(b, c):
            tail_copy(b).wait()
            return c

        lax.fori_loop(nused_ref[0], xs_hbm.shape[0] // block, start_tail, 0)
        lax.fori_loop(nused_ref[0], xs_hbm.shape[0] // block, wait_tail, 0)

        def for_each_run(e, action):
            row, length = pad_row_ref[e], pad_len_ref[e]
            for chunk in PAD_CHUNKS:
                used = length & chunk

                @pl.when(used != 0)
                def _(row=row, chunk=chunk):
                    action(pltpu.make_async_copy(zeros_sc.at[pl.ds(0, chunk)], xs_hbm.at[pl.ds(row, chunk)], zero_sem))
                row = row + used

        def start_runs(e, c):
            for_each_run(e, lambda cp: cp.start())
            return c

        def wait_runs(e, c):
            for_each_run(e, lambda cp: cp.wait())
            return c

        lax.fori_loop(0, N_EXPERTS, start_runs, 0)
        lax.fori_loop(0, N_EXPERTS, wait_runs, 0)

    def copy(t, k):
        return pltpu.make_async_copy(h_ref.at[t], xs_hbm.at[dest_ref[0, 0, 2 * t + k]], sem)

    def start(t, c):
        copy(t, 0).start()
        copy(t, 1).start()
        return c

    def wait(t, c):
        copy(t, 0).wait()
        copy(t, 1).wait()
        return c

    lax.fori_loop(0, tm, start, 0, unroll=4)
    lax.fori_loop(0, tm, wait, 0, unroll=4)


def _dispatch(h, dest, pad_row, pad_len, nused, p_rows):
    n = h.shape[0]
    tm = GATHER_TILE
    return pl.pallas_call(
        _dispatch_kernel,
        grid=(n // tm,),
        in_specs=[
            pl.BlockSpec((1, 1, 2 * tm), lambda i: (i, 0, 0), memory_space=pltpu.SMEM),
            pl.BlockSpec(memory_space=pltpu.SMEM),
            pl.BlockSpec(memory_space=pltpu.SMEM),
            pl.BlockSpec(memory_space=pltpu.SMEM),
            pl.BlockSpec((tm, ROW_SLABS, LANES), lambda i: (i, 0, 0)),
        ],
        out_specs=pl.BlockSpec(memory_space=pl.ANY),
        out_shape=jax.ShapeDtypeStruct((p_rows, ROW_SLABS, LANES), U32),
        scratch_shapes=[pltpu.VMEM((MOE_BLOCK, ROW_SLABS, LANES), U32),
                        pltpu.SemaphoreType.DMA(()), pltpu.SemaphoreType.DMA(())],
        compiler_params=_params("arbitrary"),
        name="moe_dispatch",
    )(dest.reshape(n // tm, 1, 2 * tm), pad_row, pad_len, nused, h)


def _expert_kernel(bexp_ref, nused_ref, x_ref, wg_ref, wu_ref, wd_ref, o_ref):
    del bexp_ref
    is_live = pl.program_id(0) < nused_ref[0]

    @pl.when(jnp.logical_not(is_live))
    def _():
        o_ref[...] = jnp.zeros_like(o_ref)

    @pl.when(is_live)
    def _():
        xb = _load_rows(x_ref).astype(BF16)
        g = _dot(xb, wg_ref[...].astype(BF16))
        u = _dot(xb, wu_ref[...].astype(BF16))
        a = (g * (1.0 / (1.0 + jnp.exp(-g))) * u).astype(BF16)
        _store_rows(o_ref, _dot(a, wd_ref[...].astype(BF16)))


def _experts(xs, block_exp, nused, w_gate, w_up, w_down, layer):
    nb = xs.shape[0] // MOE_BLOCK
    live = lambda i, be, nu: jnp.minimum(i, nu[0] - 1)
    wsel = lambda i, be, nu: (layer, be[live(i, be, nu)], 0, 0)
    return pl.pallas_call(
        _expert_kernel,
        grid_spec=pltpu.PrefetchScalarGridSpec(
            num_scalar_prefetch=2,
            grid=(nb,),
            in_specs=[
                pl.BlockSpec((MOE_BLOCK, ROW_SLABS, LANES), lambda i, be, nu: (live(i, be, nu), 0, 0)),
                pl.BlockSpec((None, None, D_MODEL, EXPERT_FF), wsel),
                pl.BlockSpec((None, None, D_MODEL, EXPERT_FF), wsel),
                pl.BlockSpec((None, None, EXPERT_FF, D_MODEL), wsel),
            ],
            out_specs=pl.BlockSpec((MOE_BLOCK, ROW_SLABS, LANES), lambda i, be, nu: (i, 0, 0)),
        ),
        out_shape=jax.ShapeDtypeStruct(xs.shape, U32),
        compiler_params=_params("arbitrary"),
        name="moe_experts",
    )(block_exp, nused, xs, w_gate, w_up, w_down)


def _combine_kernel(dcur_ref, dnext_ref, ys_hbm, x_ref, route_ref, o_ref, buf, sems):
    tm = GATHER_TILE
    i = pl.program_id(0)
    slot = i % 2

    def copy(d_ref, s, t, k):
        return pltpu.make_async_copy(ys_hbm.at[d_ref[0, 0, 2 * t + k]], buf.at[s, k, t], sems.at[s])

    def gather(d_ref, s):
        def start(t, c):
            copy(d_ref, s, t, 0).start()
            copy(d_ref, s, t, 1).start()
            return c
        lax.fori_loop(0, tm, start, 0, unroll=4)

    @pl.when(i == 0)
    def _():
        gather(dcur_ref, 0)

    @pl.when(i + 1 < pl.num_programs(0))
    def _():
        gather(dnext_ref, 1 - slot)

    def wait(t, c):
        copy(dcur_ref, slot, t, 0).wait()
        copy(dcur_ref, slot, t, 1).wait()
        return c

    lax.fori_loop(0, tm, wait, 0, unroll=4)
    r = route_ref[...]
    y0 = _load_rows(buf.at[slot, 0])
    y1 = _load_rows(buf.at[slot, 1])
    o_ref[...] = x_ref[...] + r[:, 4:5] * y0 + r[:, 5:6] * y1


def _combine(ys, dest, x, route):
    n = x.shape[0]
    tm = GATHER_TILE
    nt = n // tm
    dest_t = dest.reshape(nt, 1, 2 * tm)
    return pl.pallas_call(
        _combine_kernel,
        grid=(nt,),
        in_specs=[
            pl.BlockSpec((1, 1, 2 * tm), lambda i: (i, 0, 0), memory_space=pltpu.SMEM),
            pl.BlockSpec((1, 1, 2 * tm), lambda i: (jnp.minimum(i + 1, nt - 1), 0, 0), memory_space=pltpu.SMEM),
            pl.BlockSpec(memory_space=pl.ANY),
            pl.BlockSpec((tm, D_MODEL), lambda i: (i, 0)),
            pl.BlockSpec((tm, LANES), lambda i: (i, 0)),
        ],
        out_specs=pl.BlockSpec((tm, D_MODEL), lambda i: (i, 0)),
        out_shape=jax.ShapeDtypeStruct((n, D_MODEL), F32),
        scratch_shapes=[pltpu.VMEM((2, 2, tm, ROW_SLABS, LANES), U32), pltpu.SemaphoreType.DMA((2,))],
        compiler_params=_params("arbitrary"),
        name="moe_combine",
    )(dest_t, dest_t, ys, x, route)


def _moe(x_mid, h, route, counts, w_gate, w_up, w_down, layer):
    n = x_mid.shape[0]
    nb = (n * 2) // MOE_BLOCK + N_EXPERTS
    cnt = counts[0, N_GROUPS:N_GROUPS + N_EXPERTS].astype(jnp.int32)
    padded = ((cnt + MOE_BLOCK - 1) // MOE_BLOCK) * MOE_BLOCK
    pad_end = jnp.cumsum(padded)
    pad_start = pad_end - padded
    ri = route[:, 0:4].astype(jnp.int32)
    start_of = jnp.sum(jnp.where(ri[:, 0:2, None] == jnp.arange(N_EXPERTS, dtype=jnp.int32), pad_start, 0), axis=-1)
    dest = start_of + ri[:, 2:4]
    block_row = jnp.arange(nb, dtype=jnp.int32) * MOE_BLOCK
    block_exp = jnp.minimum(jnp.sum(pad_end[None, :] <= block_row[:, None], axis=1), N_EXPERTS - 1).astype(jnp.int32)
    nused = (pad_end[-1:] // MOE_BLOCK).astype(jnp.int32)
    xs = _dispatch(h, dest, pad_start + cnt, padded - cnt, nused, nb * MOE_BLOCK)
    ys = _experts(xs, block_exp, nused, w_gate, w_up, w_down, layer)
    return _combine(ys, dest, x_mid, route)


def _mlstm_in_kernel(x_ref, g_ref, w_ref, wg_hi_ref, wg_lo_ref, gb_ref, qk_ref, v_ref, og_ref, gt_ref):
    hf = _rms(x_ref[...], g_ref[...])
    h = hf.astype(BF16)
    qk_ref[...] = _dot(h, w_ref[:, 0:D_MODEL])
    v_ref[...] = _dot(h, w_ref[:, D_MODEL:2 * D_MODEL]).astype(BF16)
    og_ref[...] = _dot(h, w_ref[:, 2 * D_MODEL:3 * D_MODEL]).astype(BF16)
    gt_ref[...] = _dot3(hf, wg_hi_ref[...], wg_lo_ref[...]) + gb_ref[...]


def _mlstm_in_proj(x, gain, w, wg_hi, wg_lo, gb):
    n = x.shape[0]
    tm = WIDE_ROW_TILE
    row = lambda w_: pl.BlockSpec((tm, w_), lambda i: (i, 0))
    return pl.pallas_call(
        _mlstm_in_kernel,
        grid=(n // tm,),
        in_specs=[
            row(D_MODEL),
            _resident((1, D_MODEL)),
            _resident((D_MODEL, 3 * D_MODEL)),
            _resident((D_MODEL, LANES)),
            _resident((D_MODEL, LANES)),
            _resident((1, LANES)),
        ],
        out_specs=[row(D_MODEL), row(D_MODEL), row(D_MODEL), row(LANES)],
        out_shape=[
            jax.ShapeDtypeStruct((n, D_MODEL), F32),
            jax.ShapeDtypeStruct((n, D_MODEL), BF16),
            jax.ShapeDtypeStruct((n, D_MODEL), BF16),
            jax.ShapeDtypeStruct((n, LANES), F32),
        ],
        compiler_params=_params("parallel"),
        name="mlstm_in_proj",
    )(x, gain, w, wg_hi, wg_lo, gb)


def _log_sigmoid(x):
    return jnp.minimum(x, 0.0) - jnp.log(1.0 + jnp.exp(-jnp.abs(x)))


def _mlstm_kernel(qk_ref, v_ref, og_ref, gt_ref, cw_ref, cb_ref, hg_ref, tri_ref, y_ref,
                  cbuf, c_sc, n_sc, m_sc):
    L = MLSTM_CHUNK
    dk, dv = MLSTM_QK_DIM, MLSTM_V_DIM

    @pl.when(pl.program_id(1) == 0)
    def _():
        cbuf[0:8, :] = jnp.zeros((8, D_MODEL), F32)
        c_sc[...] = jnp.zeros_like(c_sc)
        n_sc[...] = jnp.zeros_like(n_sc)
        m_sc[...] = jnp.zeros_like(m_sc)

    cbuf[8:8 + L, :] = qk_ref[...]
    conv = cb_ref[...] + cw_ref[0:1, :] * cbuf[5:5 + L, :]
    for j in range(1, CONV_WIDTH):
        conv = conv + cw_ref[j:j + 1, :] * cbuf[5 + j:5 + j + L, :]
    cbuf[0:8, :] = cbuf[L:L + 8, :]
    qk = conv * (1.0 / (1.0 + jnp.exp(-conv)))

    gates = gt_ref[...]
    lf = _log_sigmoid(gates)
    lf_hi = lf.astype(BF16)
    lf_mid, lf_lo = _hi_lo(lf - lf_hi.astype(F32))
    tri = tri_ref[...]
    b = _dot(tri, lf_hi) + (_dot(tri, lf_mid) + _dot(tri, lf_lo))
    gates_t = gates.T
    b_t = b.T
    ti = lax.broadcasted_iota(jnp.int32, (L, L), 0)
    si = lax.broadcasted_iota(jnp.int32, (L, L), 1)
    causal = ti >= si

    for h in range(MLSTM_HEADS):
        i_col, i_row = gates[:, h:h + 1], gates_t[h:h + 1, :]
        b_col, b_row = b[:, 4 + h:5 + h], b_t[4 + h:5 + h, :]
        m_prev = m_sc[h, 0:1, 0:1]
        q = qk[:, h * dk:(h + 1) * dk]
        k = qk[:, D_MODEL // 2 + h * dk:D_MODEL // 2 + (h + 1) * dk] * (dk ** -0.5)
        qb, kb = q.astype(BF16), k.astype(BF16)
        v = v_ref[:, h * dv:(h + 1) * dv]

        dmat = jnp.where(causal, b_col - b_row + i_row, NEG)
        m_inter = b_col + m_prev
        m_t = jnp.maximum(m_inter, jnp.max(dmat, axis=-1, keepdims=True))
        s = _dot_nt(qb, kb) * jnp.exp(dmat - m_t)
        w_inter = jnp.exp(m_inter - m_t)
        num = _dot(s.astype(BF16), v) + w_inter * _dot(qb, c_sc[h].astype(BF16))
        den = jnp.sum(s, axis=-1, keepdims=True) + w_inter * jnp.sum(q * n_sc[h], axis=-1, keepdims=True)
        hh = num / jnp.maximum(jnp.abs(den), jnp.exp(-m_t))

        b_last = b_col[L - 1:L, :]
        g = b_last - b_col + i_col
        m_new = jnp.maximum(b_last + m_prev, jnp.max(g, axis=0, keepdims=True))
        decay = jnp.exp(b_last + m_prev - m_new)
        wk = jnp.exp(g - m_new) * k
        c_sc[h] = decay * c_sc[h] + _dot(wk.T.astype(BF16), v)
        n_sc[h] = decay * n_sc[h] + jnp.sum(wk, axis=0, keepdims=True)
        m_sc[h] = jnp.broadcast_to(m_new, (8, LANES))

        cols = slice(h * dv, (h + 1) * dv)
        hn = hh * lax.rsqrt(jnp.mean(hh * hh, axis=-1, keepdims=True) + NORM_EPS) * hg_ref[:, cols]
        og = og_ref[:, cols].astype(F32)
        y_ref[:, cols] = (hn * (1.0 / (1.0 + jnp.exp(-og)))).astype(BF16)


def _mlstm_cell(qk, v, og, gates, conv_w, conv_b, h_gain, batch, seq):
    L = MLSTM_CHUNK
    nc = seq // L
    row = lambda w_: pl.BlockSpec((L, w_), lambda b, c: (b * nc + c, 0))
    tri = (lax.broadcasted_iota(jnp.int32, (L, L), 0) >= lax.broadcasted_iota(jnp.int32, (L, L), 1)).astype(BF16)
    return pl.pallas_call(
        _mlstm_kernel,
        grid=(batch, nc),
        in_specs=[
            row(D_MODEL), row(D_MODEL), row(D_MODEL), row(LANES),
            _resident((CONV_WIDTH, D_MODEL)),
            _resident((1, D_MODEL)),
            _resident((1, D_MODEL)),
            _resident((L, L)),
        ],
        out_specs=row(D_MODEL),
        out_shape=jax.ShapeDtypeStruct((batch * seq, D_MODEL), BF16),
        scratch_shapes=[
            pltpu.VMEM((L + 8, D_MODEL), F32),
            pltpu.VMEM((MLSTM_HEADS, MLSTM_QK_DIM, MLSTM_V_DIM), F32),
            pltpu.VMEM((MLSTM_HEADS, 1, MLSTM_QK_DIM), F32),
            pltpu.VMEM((MLSTM_HEADS, 8, LANES), F32),
        ],
        compiler_params=_params("arbitrary", "arbitrary"),
        name="mlstm_cell",
    )(qk, v, og, gates, conv_w, conv_b, h_gain, tri)


def _pad_lanes(w):
    return jnp.pad(w, ((0, 0), (0, LANES - w.shape[1])))


def kernel(x, norm_mix, norm_ffn, attn_w_in, attn_q_norm, attn_k_norm, attn_w_out, mlstm_w_in, mlstm_conv_w, mlstm_conv_b, mlstm_gate_b, mlstm_h_norm, mlstm_w_out, moe_w_group, moe_b_group, moe_w_expert, moe_b_expert, moe_w_gate, moe_w_up, moe_w_down):
    batch, seq, d = x.shape
    assert d == D_MODEL and seq % (ATTN_BLOCK * max(DILATIONS)) == 0 and seq % MLSTM_CHUNK == 0
    assert (batch * seq) % GATHER_TILE == 0 and GATHER_TILE % ROW_TILE == 0
    depth = norm_mix.shape[0]
    n = batch * seq
    xf = x.reshape(n, d)

    head_of = jnp.arange(D_MODEL) // ATTN_HEAD_DIM
    bd = (head_of[:, None] == head_of[None, :]).astype(BF16)
    expand = (jnp.arange(LANES)[:, None] == head_of[None, :]).astype(BF16)

    for layer in range(depth):
        j = layer // N_MIXERS
        gain = norm_mix[layer].reshape(1, d)
        wr = jnp.concatenate([moe_w_group[layer], moe_w_expert[layer]], axis=1)
        wr_hi, wr_lo = _hi_lo(_pad_lanes(wr))
        rb = _pad_lanes(jnp.concatenate([moe_b_group[layer], moe_b_expert[layer]]).reshape(1, -1))
        ffn_gain = norm_ffn[layer].reshape(1, d)
        if layer % N_MIXERS == 0:
            qkg = jnp.concatenate([jnp.tile(attn_q_norm[j], ATTN_HEADS) * ATTN_HEAD_DIM ** -0.5,
                                   jnp.tile(attn_k_norm[j], ATTN_HEADS)]).reshape(1, 2 * d)
            qkvs = _attn_in_proj(xf, gain, attn_w_in[j].astype(BF16), bd, qkg)
            outs = [_attn_branch(qkv_d, batch, seq, dil) for qkv_d, dil in zip(qkvs, DILATIONS)]
            mix = tuple(o for o, _ in outs) + tuple(l for _, l in outs)
            x_mid, h, route, counts = _post_mixer(mix, xf, attn_w_out[j].astype(BF16), ffn_gain,
                                                  wr_hi, wr_lo, rb, expand=expand)
        else:
            w_in = mlstm_w_in[j]
            wg_hi, wg_lo = _hi_lo(_pad_lanes(w_in[:, 3 * d:]))
            gb = _pad_lanes(mlstm_gate_b[j].reshape(1, -1))
            qk, v, og, gates = _mlstm_in_proj(xf, gain, w_in[:, :3 * d].astype(BF16), wg_hi, wg_lo, gb)
            y = _mlstm_cell(qk, v, og, gates, mlstm_conv_w[j], mlstm_conv_b[j].reshape(1, d),
                            mlstm_h_norm[j].reshape(1, d), batch, seq)
            x_mid, h, route, counts = _post_mixer((y,), xf, mlstm_w_out[j].astype(BF16), ffn_gain,
                                                  wr_hi, wr_lo, rb)
        xf = _moe(x_mid, h, route, counts, moe_w_gate, moe_w_up, moe_w_down, layer)
    return xf.reshape(batch, seq, d)
```
